```python
import math
import jax, jax.numpy as jnp
from jax import lax
import numpy as np

D_MODEL = 1024
BATCH = 32
SEQ = 2048
DEPTH = 4
DEC_BATCH = 8
DEC_SEQ = 2048
PAST_LEN = 128

GRID_W = 64
HEAD_DIM = 64
N_MIXERS = 4
GROUP_W = D_MODEL // N_MIXERS
N_HEADS_G = GROUP_W // HEAD_DIM
N_KV = 2
REP = N_HEADS_G // N_KV
KV_W = N_KV * HEAD_DIM
D_MIX = N_MIXERS * GROUP_W
RWKV_DECAY_RANK = 64
RWKV_A_RANK = 64
RWKV_SHIFT_W = 3 * GROUP_W + RWKV_DECAY_RANK + RWKV_A_RANK
LRU_C = 8.0
LRU_BLOCKS = 4
LRU_BLK = GROUP_W // LRU_BLOCKS
CONV_W = 4
CONV_LEFT = 2
Q_BLOCK = 128
WINDOW = 128
ROPE_THETA = 10000.0
NORM_EPS = 1e-6
GN_EPS = 64e-5
NEG = -1e30
A_W = RWKV_SHIFT_W + GROUP_W
B_W = GROUP_W + 2 * KV_W + GROUP_W
C_W = 2 * GROUP_W
D_W = GROUP_W + 2 * KV_W + GROUP_W
D_IN = A_W + B_W + C_W + D_W

kernel_name = "hybrid_bidir_parallel_heads_encoder"


def _rmsnorm(x, g, eps=NORM_EPS):
    xf = x.astype(jnp.float32)
    return xf * lax.rsqrt(jnp.mean(xf * xf, -1, keepdims=True) + eps) * g.astype(jnp.float32)


def _rwkv_scan(r, w, k, v, kk, a, reverse):
    xs = tuple(jnp.moveaxis(t, 1, 0) for t in (r, w, k, v, kk, a))
    bsz, nh, n = r.shape[0], r.shape[2], r.shape[3]

    def step(S, inp):
        rt, wt, kt, vt, kkt, at = inp
        sk = jnp.einsum('bhij,bhj->bhi', S, kkt)
        S = S * wt[:, :, None, :] - sk[..., None] * (kkt * at)[:, :, None, :] + vt[..., None] * kt[:, :, None, :]
        return S, jnp.einsum('bhij,bhj->bhi', S, rt)

    S0 = jnp.zeros((bsz, nh, n, n), jnp.float32)
    _, y = lax.scan(step, S0, xs, reverse=reverse)
    return jnp.moveaxis(y, 0, 1)


def _rwkv_mixer(z, shift, w0, w_up, a0, a_up, k_k, k_a, r_k, ln_g, ln_b):
    b_, t_, _ = z.shape
    xs = z[..., :RWKV_SHIFT_W]
    gate = z[..., RWKV_SHIFT_W:]
    prev = jnp.pad(xs, ((0, 0), (1, 0), (0, 0)))[:, :t_]
    nxt = jnp.pad(xs, ((0, 0), (0, 1), (0, 0)))[:, 1:]
    xs = xs + shift[0] * (prev - xs) + shift[1] * (nxt - xs)
    r = xs[..., :GROUP_W]
    k = xs[..., GROUP_W:2 * GROUP_W]
    v = xs[..., 2 * GROUP_W:3 * GROUP_W]
    wl = jnp.tanh(xs[..., 3 * GROUP_W:3 * GROUP_W + RWKV_DECAY_RANK])
    al = xs[..., 3 * GROUP_W + RWKV_DECAY_RANK:]
    hd = lambda t: t.reshape(b_, t_, N_HEADS_G, HEAD_DIM)
    kk = hd(k * k_k)
    kk = kk / jnp.maximum(jnp.sqrt(jnp.sum(kk * kk, -1, keepdims=True)), 1e-12)
    y = jnp.zeros((b_, t_, N_HEADS_G, HEAD_DIM), jnp.float32)
    ksum = jnp.zeros_like(k)
    for d, rev in ((0, False), (1, True)):
        logw = -jnp.exp(-jax.nn.softplus(-(w0[d] + wl @ w_up[d])) - 0.5)
        a = jax.nn.sigmoid(a0[d] + al @ a_up[d])
        kd = k * (1.0 + (a - 1.0) * k_a)
        y = y + _rwkv_scan(hd(r), hd(jnp.exp(logw)), hd(kd), hd(v), kk, hd(a), rev)
        ksum = ksum + kd
    mu = jnp.mean(y, -1, keepdims=True)
    var = jnp.mean(jnp.square(y - mu), -1, keepdims=True)
    y = ((y - mu) * lax.rsqrt(var + GN_EPS)).reshape(b_, t_, GROUP_W) * ln_g + ln_b
    bonus = jnp.sum(hd(r) * hd(ksum) * r_k, -1, keepdims=True) * hd(v)
    y = y + bonus.reshape(b_, t_, GROUP_W)
    return y * jax.nn.silu(gate)


def _axial_angles(t_):
    rows = t_ // GRID_W
    row = jnp.repeat(jnp.arange(rows), GRID_W).astype(jnp.float32)
    col = jnp.tile(jnp.arange(GRID_W), rows).astype(jnp.float32)
    half = HEAD_DIM // 2
    inv = ROPE_THETA ** (-jnp.arange(0, half, 2, dtype=jnp.float32) / half)
    return row[:, None] * inv, col[:, None] * inv


def _rope_half(x, ang):
    n = x.shape[-1] // 2
    c = jnp.cos(ang)[:, None, :]
    s = jnp.sin(ang)[:, None, :]
    x1, x2 = x[..., :n], x[..., n:]
    return jnp.concatenate([x1 * c - x2 * s, x1 * s + x2 * c], -1)


def _axial_rope(x, ang_r, ang_c):
    h = HEAD_DIM // 2
    return jnp.concatenate([_rope_half(x[..., :h], ang_r), _rope_half(x[..., h:], ang_c)], -1)


def _global_attn_mixer(z, q_norm, k_norm):
    b_, t_, _ = z.shape
    q = z[..., :GROUP_W].reshape(b_, t_, N_HEADS_G, HEAD_DIM)
    k = z[..., GROUP_W:GROUP_W + KV_W].reshape(b_, t_, N_KV, HEAD_DIM)
    v = z[..., GROUP_W + KV_W:GROUP_W + 2 * KV_W].reshape(b_, t_, N_KV, HEAD_DIM)
    gate = z[..., GROUP_W + 2 * KV_W:]
    ang_r, ang_c = _axial_angles(t_)
    q = _axial_rope(_rmsnorm(q, q_norm), ang_r, ang_c)
    k = _axial_rope(_rmsnorm(k, k_norm), ang_r, ang_c)
    scale = HEAD_DIM ** -0.5
    nb = t_ // Q_BLOCK
    qb = q.reshape(b_, nb, Q_BLOCK, N_KV, REP, HEAD_DIM).transpose(1, 0, 2, 3, 4, 5)

    def block(qi):
        s = jnp.einsum('bqgrd,bkgd->bgrqk', qi, k) * scale
        p = jax.nn.softmax(s, axis=-1)
        return jnp.einsum('bgrqk,bkgd->bqgrd', p, v)

    o = lax.map(block, qb)
    o = o.transpose(1, 0, 2, 3, 4, 5).reshape(b_, t_, GROUP_W)
    return o * jax.nn.silu(gate)


def _lin_comb(e1, e2):
    a1, b1 = e1
    a2, b2 = e2
    return a1 * a2, a2 * b1 + b2


def _rglru_mixer(z, conv_w, conv_b, gate_w, gate_b, lam):
    b_, t_, _ = z.shape
    xb = z[..., :GROUP_W]
    gate = z[..., GROUP_W:]
    xp = jnp.pad(xb, ((0, 0), (CONV_LEFT, CONV_W - 1 - CONV_LEFT), (0, 0)))
    xc = conv_b + sum(conv_w[j] * xp[:, j:j + t_] for j in range(CONV_W))
    xh = xc.reshape(b_, t_, LRU_BLOCKS, LRU_BLK)
    h = jnp.zeros_like(xc)
    for d, rev in ((0, False), (1, True)):
        g = jnp.einsum('btnd,knde->kbtne', xh, gate_w[d]).reshape(2, b_, t_, GROUP_W) + gate_b[d][:, None, None, :]
        r = jax.nn.sigmoid(g[0])
        i = jax.nn.sigmoid(g[1])
        log_a = -LRU_C * r * jax.nn.softplus(-lam[d])
        a = jnp.exp(log_a)
        bterm = jnp.sqrt(-jnp.expm1(2.0 * log_a)) * (i * xc)
        _, hd = lax.associative_scan(_lin_comb, (a, bterm), axis=1, reverse=rev)
        h = h + hd
    return h * jax.nn.silu(gate)


def _window_attn_mixer(z, sink):
    b_, t_, _ = z.shape
    q = z[..., :GROUP_W].reshape(b_, t_, N_KV, REP, HEAD_DIM)
    k = z[..., GROUP_W:GROUP_W + KV_W].reshape(b_, t_, N_KV, HEAD_DIM)
    v = z[..., GROUP_W + KV_W:GROUP_W + 2 * KV_W].reshape(b_, t_, N_KV, HEAD_DIM)
    gate = z[..., GROUP_W + 2 * KV_W:]
    nb = t_ // Q_BLOCK
    pad = ((0, 0), (WINDOW, WINDOW), (0, 0), (0, 0))
    kp = jnp.pad(k, pad).reshape(b_, nb + 2, Q_BLOCK, N_KV, HEAD_DIM)
    vp = jnp.pad(v, pad).reshape(b_, nb + 2, Q_BLOCK, N_KV, HEAD_DIM)
    kw = jnp.concatenate([kp[:, :-2], kp[:, 1:-1], kp[:, 2:]], axis=2)
    vw = jnp.concatenate([vp[:, :-2], vp[:, 1:-1], vp[:, 2:]], axis=2)
    qb = q.reshape(b_, nb, Q_BLOCK, N_KV, REP, HEAD_DIM)
    s = jnp.einsum('bnqgrd,bnkgd->bngrqk', qb, kw) * (HEAD_DIM ** -0.5)
    qpos = jnp.arange(nb)[:, None] * Q_BLOCK + jnp.arange(Q_BLOCK)[None]
    kpos = jnp.arange(nb)[:, None] * Q_BLOCK - WINDOW + jnp.arange(3 * Q_BLOCK)[None]
    dist = jnp.abs(kpos[:, None, :] - qpos[:, :, None])
    valid = (dist <= WINDOW) & (kpos >= 0)[:, None, :] & (kpos < t_)[:, None, :]
    slopes = jnp.exp2(-8.0 * jnp.arange(1, N_HEADS_G + 1, dtype=jnp.float32) / N_HEADS_G).reshape(N_KV, REP)
    bias = -slopes[None, :, :, None, None] * dist[:, None, None].astype(jnp.float32)
    s = jnp.where(valid[None, :, None, None], s + bias[None], NEG)
    sink_l = sink.astype(jnp.float32).reshape(N_KV, REP)[None, None, :, :, None, None]
    m = jnp.maximum(jnp.max(s, -1, keepdims=True), sink_l)
    p = jnp.exp(s - m)
    p = p / (jnp.sum(p, -1, keepdims=True) + jnp.exp(sink_l - m))
    o = jnp.einsum('bngrqk,bnkgd->bnqgrd', p, vw).reshape(b_, t_, GROUP_W)
    return o * jax.nn.silu(gate)


def _trunk(x, p):
    for l in range(DEPTH):
        h = _rmsnorm(x, p['norm_g'][l])
        zz = h @ p['w_in'][l].astype(jnp.float32)
        zA = zz[..., :A_W]
        zB = zz[..., A_W:A_W + B_W]
        zC = zz[..., A_W + B_W:A_W + B_W + C_W]
        zD = zz[..., A_W + B_W + C_W:]
        oA = _rwkv_mixer(zA, p['rwkv_shift'][l], p['rwkv_w0'][l], p['rwkv_w_up'][l], p['rwkv_a0'][l],
                         p['rwkv_a_up'][l], p['rwkv_k_k'][l], p['rwkv_k_a'][l], p['rwkv_r_k'][l],
                         p['rwkv_ln_g'][l], p['rwkv_ln_b'][l])
        oB = _global_attn_mixer(zB, p['attn_q_norm'][l], p['attn_k_norm'][l])
        oC = _rglru_mixer(zC, p['lru_conv_w'][l], p['lru_conv_b'][l], p['lru_gate_w'][l],
                          p['lru_gate_b'][l], p['lru_lambda'][l])
        oD = _window_attn_mixer(zD, p['swa_sink'][l])
        o = jnp.concatenate([oA, oB, oC, oD], -1) @ p['w_out'][l].astype(jnp.float32)
        x = x + o.astype(x.dtype)
    return _rmsnorm(x, p['final_g']).astype(x.dtype)


def setup_inputs(seed: int = 0) -> dict:
    key = jax.random.key(seed)
    ks = jax.random.split(key, 24)
    L, G = DEPTH, GROUP_W
    nrm = lambda k, s: jax.random.normal(k, s, jnp.float32)
    a_target = jax.random.uniform(ks[20], (L, 2, G), jnp.float32, 0.9, 0.999)
    a_base = a_target ** (1.0 / LRU_C)
    lam = jnp.log(a_base) - jnp.log1p(-a_base)
    return {
        "x_prompt": nrm(ks[0], (BATCH, SEQ, D_MODEL)),
        "x_sample": nrm(ks[1], (DEC_BATCH, DEC_SEQ, D_MODEL)),
        "norm_g": 1.0 + 0.02 * nrm(ks[2], (L, D_MODEL)),
        "w_in": nrm(ks[3], (L, D_MODEL, D_IN)) * D_MODEL ** -0.5,
        "w_out": nrm(ks[4], (L, D_MIX, D_MODEL)) * D_MIX ** -0.5,
        "rwkv_shift": jax.random.uniform(ks[5], (L, 2, RWKV_SHIFT_W), jnp.float32, 0.0, 0.5),
        "rwkv_w0": jax.random.uniform(ks[6], (L, 2, G), jnp.float32, -6.0, 1.0),
        "rwkv_w_up": 0.1 * nrm(ks[7], (L, 2, RWKV_DECAY_RANK, G)),
        "rwkv_a0": 0.5 * nrm(ks[8], (L, 2, G)),
        "rwkv_a_up": 0.5 * RWKV_A_RANK ** -0.5 * nrm(ks[9], (L, 2, RWKV_A_RANK, G)),
        "rwkv_k_k": 0.85 + 0.02 * nrm(ks[10], (L, G)),
        "rwkv_k_a": 1.0 + 0.02 * nrm(ks[11], (L, G)),
        "rwkv_r_k": 0.1 * nrm(ks[12], (L, N_HEADS_G, HEAD_DIM)),
        "rwkv_ln_g": 1.0 + 0.02 * nrm(ks[13], (L, G)),
        "rwkv_ln_b": 0.02 * nrm(ks[14], (L, G)),
        "attn_q_norm": 1.0 + 0.02 * nrm(ks[15], (L, HEAD_DIM)),
        "attn_k_norm": 1.0 + 0.02 * nrm(ks[16], (L, HEAD_DIM)),
        "lru_conv_w": CONV_W ** -0.5 * nrm(ks[17], (L, CONV_W, G)),
        "lru_conv_b": 0.02 * nrm(ks[18], (L, G)),
        "lru_gate_w": LRU_BLK ** -0.5 * nrm(ks[19], (L, 2, 2, LRU_BLOCKS, LRU_BLK, LRU_BLK)),
        "lru_gate_b": 0.02 * nrm(ks[21], (L, 2, 2, G)),
        "lru_lambda": lam,
        "swa_sink": nrm(ks[22], (L, N_HEADS_G)),
        "final_g": 1.0 + 0.02 * nrm(ks[23], (D_MODEL,)),
    }


def reference(x_prompt, x_sample, norm_g, w_in, w_out, rwkv_shift, rwkv_w0, rwkv_w_up, rwkv_a0,
              rwkv_a_up, rwkv_k_k, rwkv_k_a, rwkv_r_k, rwkv_ln_g, rwkv_ln_b, attn_q_norm, attn_k_norm,
              lru_conv_w, lru_conv_b, lru_gate_w, lru_gate_b, lru_lambda, swa_sink, final_g):
    p = dict(norm_g=norm_g, w_in=w_in, w_out=w_out, rwkv_shift=rwkv_shift, rwkv_w0=rwkv_w0,
             rwkv_w_up=rwkv_w_up, rwkv_a0=rwkv_a0, rwkv_a_up=rwkv_a_up, rwkv_k_k=rwkv_k_k,
             rwkv_k_a=rwkv_k_a, rwkv_r_k=rwkv_r_k, rwkv_ln_g=rwkv_ln_g, rwkv_ln_b=rwkv_ln_b,
             attn_q_norm=attn_q_norm, attn_k_norm=attn_k_norm, lru_conv_w=lru_conv_w,
             lru_conv_b=lru_conv_b, lru_gate_w=lru_gate_w, lru_gate_b=lru_gate_b,
             lru_lambda=lru_lambda, swa_sink=swa_sink, final_g=final_g)
    y_prompt = _trunk(x_prompt, p)
    y_sample = _trunk(x_sample, p)
    return (y_prompt, y_sample)
```

```python
import functools
import math

import jax
import jax.numpy as jnp
import numpy as np
from jax import lax
from jax.experimental import pallas as pl
from jax.experimental.pallas import tpu as pltpu

D_MODEL = 1024
DEPTH = 4
GRID_W = 64
HEAD_DIM = 64
GROUP_W = 256
N_HEADS_G = 4
N_KV = 2
REP = 2
KV_W = 128
RWKV_DECAY_RANK = 64
RWKV_SHIFT_W = 896
LRU_C = 8.0
LRU_BLOCKS = 4
LRU_BLK = 64
CONV_W = 4
CONV_LEFT = 2
Q_BLOCK = 128
WINDOW = 128
ROPE_THETA = 10000.0
NORM_EPS = 1e-6
GN_EPS = 64e-5
NEG = -1e30
A_W = 1152
B_W = 768
C_W = 512
D_W = 768

CHUNK = 64
ROW_TILE = 512
HALO = 8
V7X_VMEM_LIMIT = 56 * 1024 * 1024
F32 = jnp.float32
BF16 = jnp.bfloat16
HI = lax.Precision.HIGHEST
EXP_M05 = math.exp(-0.5)


def _dot(a, b, prec=None):
    return jnp.dot(a, b, preferred_element_type=F32, precision=prec)


def _dot_nt(a, b, prec=None):
    return lax.dot_general(a, b, (((1,), (1,)), ((), ())), preferred_element_type=F32, precision=prec)


def _dot_tn(a, b, prec=None):
    return lax.dot_general(a, b, (((0,), (0,)), ((), ())), preferred_element_type=F32, precision=prec)


def _sigmoid(x):
    return 1.0 / (1.0 + jnp.exp(-x))


def _silu(x):
    return x * _sigmoid(x)


def _rms(x, g):
    return x * lax.rsqrt(jnp.mean(x * x, axis=-1, keepdims=True) + NORM_EPS) * g


def _params(n_axes=1):
    return pltpu.CompilerParams(dimension_semantics=("arbitrary",) * n_axes,
                                vmem_limit_bytes=V7X_VMEM_LIMIT)


def _full(a):
    nd = a.ndim
    return pl.BlockSpec(a.shape, lambda *_: (0,) * nd)


def _norm_kernel(x_ref, g_ref, h_ref):
    h_ref[...] = _rms(x_ref[...], g_ref[...]).astype(BF16)


def _norm(x2, g):
    m = x2.shape[0]
    tm = min(ROW_TILE, m)
    return pl.pallas_call(
        _norm_kernel,
        grid=(m // tm,),
        in_specs=[pl.BlockSpec((tm, D_MODEL), lambda i: (i, 0)), _full(g)],
        out_specs=pl.BlockSpec((tm, D_MODEL), lambda i: (i, 0)),
        out_shape=jax.ShapeDtypeStruct((m, D_MODEL), BF16),
        compiler_params=_params(),
        name="norm",
    )(x2, g)


def _out_kernel(oa_ref, ob_ref, oc_ref, od_ref, w_ref, x_ref, g_ref, *out_refs, final):
    acc = _dot(oa_ref[...], w_ref[0:GROUP_W, :])
    acc += _dot(ob_ref[...], w_ref[GROUP_W:2 * GROUP_W, :])
    acc += _dot(oc_ref[...], w_ref[2 * GROUP_W:3 * GROUP_W, :])
    acc += _dot(od_ref[...], w_ref[3 * GROUP_W:4 * GROUP_W, :])
    xn = x_ref[...] + acc
    if final:
        out_refs[0][...] = _rms(xn, g_ref[...])
    else:
        out_refs[0][...] = xn
        out_refs[1][...] = _rms(xn, g_ref[...]).astype(BF16)


def _out_proj(oa, ob, oc, od, w, x2, g, final):
    m = x2.shape[0]
    tm = min(ROW_TILE, m)
    ospec = pl.BlockSpec((tm, GROUP_W), lambda i: (i, 0))
    xspec = pl.BlockSpec((tm, D_MODEL), lambda i: (i, 0))
    if final:
        out_shape = (jax.ShapeDtypeStruct((m, D_MODEL), F32),)
        out_specs = (xspec,)
    else:
        out_shape = (jax.ShapeDtypeStruct((m, D_MODEL), F32), jax.ShapeDtypeStruct((m, D_MODEL), BF16))
        out_specs = (xspec, xspec)
    return pl.pallas_call(
        functools.partial(_out_kernel, final=final),
        grid=(m // tm,),
        in_specs=[ospec, ospec, ospec, ospec, _full(w), xspec, _full(g)],
        out_specs=out_specs,
        out_shape=out_shape,
        compiler_params=_params(),
        name="out_proj",
    )(oa, ob, oc, od, w, x2, g)


def _head_blockdiag():
    i = np.arange(GROUP_W)
    return (i[:, None] // HEAD_DIM == i[None, :] // HEAD_DIM).astype(np.float32)


def _rope_tables(t_):
    rows = t_ // GRID_W
    row = np.repeat(np.arange(rows), GRID_W).astype(np.float32)
    col = np.tile(np.arange(GRID_W), rows).astype(np.float32)
    half = HEAD_DIM // 2
    inv = (ROPE_THETA ** (-np.arange(0, half, 2, dtype=np.float32) / half)).astype(np.float32)
    ang_r = row[:, None] * inv
    ang_c = col[:, None] * inv
    cos = np.concatenate([np.cos(ang_r), np.cos(ang_r), np.cos(ang_c), np.cos(ang_c)], -1)
    sin = np.concatenate([-np.sin(ang_r), np.sin(ang_r), -np.sin(ang_c), np.sin(ang_c)], -1)
    return (np.tile(cos, (1, 2)).astype(np.float32), np.tile(sin, (1, 2)).astype(np.float32))


def _kv_select():
    sel = np.zeros((N_HEADS_G, KV_W, GROUP_W), np.float32)
    for h in range(N_HEADS_G):
        g = h // REP
        for d in range(HEAD_DIM):
            sel[h, g * HEAD_DIM + d, h * HEAD_DIM + d] = 1.0
    return sel


def _swap16(x):
    n = x.shape[-1]
    up = pltpu.roll(x, n - 16, axis=1)
    dn = pltpu.roll(x, 16, axis=1)
    lane = lax.broadcasted_iota(jnp.int32, x.shape, 1)
    return jnp.where((lane % 32) < 16, up, dn)


def _gattn_kernel(h_ref, w_ref, qw_ref, kw_ref, cos_ref, sin_ref, seg_ref, sel_ref, o_ref,
                  q_s, kw_s, vw_s, *, t_, tq, tr):
    seg = seg_ref[...]

    def prologue(i, carry):
        r0 = pl.multiple_of(i * tr, tr)
        rows = pl.ds(r0, tr)
        h = h_ref[0, rows, :]
        cos = cos_ref[rows, :]
        sin = sin_ref[rows, :]

        def normrope(z, wgt):
            ms = _dot(z * z, seg, HI) * (1.0 / HEAD_DIM)
            z = z * lax.rsqrt(ms + NORM_EPS) * wgt
            return z * cos + _swap16(z) * sin

        for half in range(2):
            zq = _dot(h, w_ref[:, half * 128:(half + 1) * 128])
            qr = normrope(zq, qw_ref[...]) * (HEAD_DIM ** -0.5)
            q_s[rows, half * 128:(half + 1) * 128] = qr.astype(BF16)
        zk = _dot(h, w_ref[:, GROUP_W:GROUP_W + KV_W])
        kr = normrope(zk, kw_ref[...]).astype(BF16)
        zv = _dot(h, w_ref[:, GROUP_W + KV_W:GROUP_W + 2 * KV_W]).astype(BF16)
        for hd in range(N_HEADS_G):
            kw_s[hd, rows, :] = _dot(kr, sel_ref[hd]).astype(BF16)
            vw_s[hd, rows, :] = _dot(zv, sel_ref[hd]).astype(BF16)
        return carry

    lax.fori_loop(0, t_ // tr, prologue, 0)

    def qblock(i, carry):
        r0 = pl.multiple_of(i * tq, tq)
        q = q_s[pl.ds(r0, tq), :]
        acc = jnp.zeros((tq, GROUP_W), F32)
        for hd in range(N_HEADS_G):
            s = _dot_nt(q, kw_s[hd])
            m = jnp.max(s, axis=-1, keepdims=True)
            p = jnp.exp(s - m)
            l = jnp.sum(p, axis=-1, keepdims=True)
            acc += _dot(p.astype(BF16), vw_s[hd]) * (1.0 / l)
        gate = _dot(h_ref[0, pl.ds(r0, tq), :], w_ref[:, GROUP_W + 2 * KV_W:])
        o_ref[0, pl.ds(r0, tq), :] = (acc * _silu(gate)).astype(BF16)
        return carry

    lax.fori_loop(0, t_ // tq, qblock, 0)


def _gattn(h3, w, qw, kw, tabs):
    b_, t_, _ = h3.shape
    tq = min(256, t_)
    cos, sin, seg, sel = tabs["cos"], tabs["sin"], tabs["seg128"], tabs["sel"]
    return pl.pallas_call(
        functools.partial(_gattn_kernel, t_=t_, tq=tq, tr=min(ROW_TILE, t_)),
        grid=(b_,),
        in_specs=[pl.BlockSpec((1, t_, D_MODEL), lambda b: (b, 0, 0)), _full(w), _full(qw), _full(kw),
                  _full(cos), _full(sin), _full(seg), _full(sel)],
        out_specs=pl.BlockSpec((1, t_, GROUP_W), lambda b: (b, 0, 0)),
        out_shape=jax.ShapeDtypeStruct((b_, t_, GROUP_W), BF16),
        scratch_shapes=[pltpu.VMEM((t_, GROUP_W), BF16),
                        pltpu.VMEM((N_HEADS_G, t_, GROUP_W), BF16),
                        pltpu.VMEM((N_HEADS_G, t_, GROUP_W), BF16)],
        compiler_params=_params(),
        name="gattn",
    )(h3, w, qw, kw, cos, sin, seg, sel)


def _wattn_kernel(sink_ref, h_ref, w_ref, bias_ref, sel_ref, o_ref, q_s, kp_s, vp_s, *, t_, tr):
    nb = t_ // Q_BLOCK
    zpad = jnp.zeros((WINDOW, GROUP_W), BF16)
    for hd in range(N_HEADS_G):
        kp_s[hd, 0:WINDOW, :] = zpad
        kp_s[hd, WINDOW + t_:, :] = zpad
        vp_s[hd, 0:WINDOW, :] = zpad
        vp_s[hd, WINDOW + t_:, :] = zpad

    def prologue(i, carry):
        r0 = pl.multiple_of(i * tr, tr)
        rows = pl.ds(r0, tr)
        prow = pl.ds(pl.multiple_of(WINDOW + r0, WINDOW), tr)
        h = h_ref[0, rows, :]
        q_s[rows, :] = (_dot(h, w_ref[:, 0:GROUP_W]) * (HEAD_DIM ** -0.5)).astype(BF16)
        zk = _dot(h, w_ref[:, GROUP_W:GROUP_W + KV_W]).astype(BF16)
        zv = _dot(h, w_ref[:, GROUP_W + KV_W:GROUP_W + 2 * KV_W]).astype(BF16)
        for hd in range(N_HEADS_G):
            kp_s[hd, prow, :] = _dot(zk, sel_ref[hd]).astype(BF16)
            vp_s[hd, prow, :] = _dot(zv, sel_ref[hd]).astype(BF16)
        return carry

    lax.fori_loop(0, t_ // tr, prologue, 0)

    kw = 3 * Q_BLOCK
    col = lax.broadcasted_iota(jnp.int32, (Q_BLOCK, kw), 1)
    row = lax.broadcasted_iota(jnp.int32, (Q_BLOCK, kw), 0)
    band = jnp.abs(col - WINDOW - row) <= WINDOW

    def qblock(i, carry):
        r0 = pl.multiple_of(i * Q_BLOCK, Q_BLOCK)
        q = q_s[pl.ds(r0, Q_BLOCK), :]
        kpos = col + (i - 1) * Q_BLOCK
        valid = band & (kpos >= 0) & (kpos < t_)
        acc = jnp.zeros((Q_BLOCK, GROUP_W), F32)
        for hd in range(N_HEADS_G):
            s = _dot_nt(q, kp_s[hd, pl.ds(r0, kw), :])
            s = jnp.where(valid, s + bias_ref[hd], NEG)
            sk = sink_ref[hd]
            m = jnp.maximum(jnp.max(s, axis=-1, keepdims=True), sk)
            p = jnp.exp(s - m)
            den = jnp.sum(p, axis=-1, keepdims=True) + jnp.exp(sk - m)
            acc += _dot(p.astype(BF16), vp_s[hd, pl.ds(r0, kw), :]) * (1.0 / den)
        gate = _dot(h_ref[0, pl.ds(r0, Q_BLOCK), :], w_ref[:, GROUP_W + 2 * KV_W:])
        o_ref[0, pl.ds(r0, Q_BLOCK), :] = (acc * _silu(gate)).astype(BF16)
        return carry

    lax.fori_loop(0, nb, qblock, 0)


def _wattn(h3, w, sink, tabs):
    b_, t_, _ = h3.shape
    bias, sel = tabs["wbias"], tabs["sel"]
    return pl.pallas_call(
        functools.partial(_wattn_kernel, t_=t_, tr=min(ROW_TILE, t_)),
        grid=(b_,),
        in_specs=[pl.BlockSpec(memory_space=pltpu.SMEM),
                  pl.BlockSpec((1, t_, D_MODEL), lambda b: (b, 0, 0)), _full(w), _full(bias), _full(sel)],
        out_specs=pl.BlockSpec((1, t_, GROUP_W), lambda b: (b, 0, 0)),
        out_shape=jax.ShapeDtypeStruct((b_, t_, GROUP_W), BF16),
        scratch_shapes=[pltpu.VMEM((t_, GROUP_W), BF16),
                        pltpu.VMEM((N_HEADS_G, t_ + 2 * WINDOW, GROUP_W), BF16),
                        pltpu.VMEM((N_HEADS_G, t_ + 2 * WINDOW, GROUP_W), BF16)],
        compiler_params=_params(),
        name="wattn",
    )(sink, h3, w, bias, sel)


def _tile_scan(a, b, rev):
    sub = lax.broadcasted_iota(jnp.int32, a.shape, 0)
    for s in (1, 2, 4):
        if rev:
            a_sh = pltpu.roll(a, 8 - s, axis=0)
            b_sh = pltpu.roll(b, 8 - s, axis=0)
            ok = sub < 8 - s
        else:
            a_sh = pltpu.roll(a, s, axis=0)
            b_sh = pltpu.roll(b, s, axis=0)
            ok = sub >= s
        b = jnp.where(ok, a * b_sh + b, b)
        a = jnp.where(ok, a * a_sh, a)
    return a, b


def _lru_kernel(h_ref, w_ref, cw_ref, cb_ref, gw_ref, gb_ref, lam_ref, o_ref, xp_s, a_s, b_s, hs_s,
                *, t_, tr):
    zhalo = jnp.zeros((HALO, GROUP_W), F32)
    xp_s[0:HALO, :] = zhalo
    xp_s[HALO + t_:, :] = zhalo

    def proj(i, carry):
        r0 = pl.multiple_of(i * tr, tr)
        xp_s[pl.ds(pl.multiple_of(HALO + r0, HALO), tr), :] = _dot(h_ref[0, pl.ds(r0, tr), :], w_ref[:, 0:GROUP_W])
        return carry

    lax.fori_loop(0, t_ // tr, proj, 0)

    lam = lam_ref[...]
    sp = jnp.maximum(-lam, 0.0) + jnp.log(1.0 + jnp.exp(-jnp.abs(lam)))

    def gates(i, carry):
        r0 = pl.multiple_of(i * tr, tr)
        rows = pl.ds(r0, tr)
        win = xp_s[pl.ds(r0, tr + 2 * HALO), :]
        xc = cb_ref[...]
        for j in range(CONV_W):
            st = HALO + j - CONV_LEFT
            xc = xc + cw_ref[j:j + 1, :] * win[st:st + tr, :]
        g = _dot(xc.astype(BF16), gw_ref[...]) + gb_ref[...]
        for d in range(2):
            r = _sigmoid(g[:, (2 * d) * GROUP_W:(2 * d + 1) * GROUP_W])
            ig = _sigmoid(g[:, (2 * d + 1) * GROUP_W:(2 * d + 2) * GROUP_W])
            a = jnp.exp(-LRU_C * r * sp[d:d + 1, :])
            a_s[d, rows, :] = a
            b_s[d, rows, :] = jnp.sqrt(1.0 - a * a) * (ig * xc)
        return carry

    lax.fori_loop(0, t_ // tr, gates, 0)

    nt = t_ // 8

    def tile(i, carry):
        cf, cr = carry
        rf = pl.multiple_of(i * 8, 8)
        rr = pl.multiple_of((nt - 1 - i) * 8, 8)
        af, bf = _tile_scan(a_s[0, pl.ds(rf, 8), :], b_s[0, pl.ds(rf, 8), :], False)
        ar, br = _tile_scan(a_s[1, pl.ds(rr, 8), :], b_s[1, pl.ds(rr, 8), :], True)
        hf = bf + af * cf
        hr = br + ar * cr
        hs_s[0, pl.ds(rf, 8), :] = hf
        hs_s[1, pl.ds(rr, 8), :] = hr
        return hf[7:8, :], hr[0:1, :]

    z = jnp.zeros((1, GROUP_W), F32)
    lax.fori_loop(0, nt, tile, (z, z))

    def epilogue(i, carry):
        r0 = pl.multiple_of(i * tr, tr)
        rows = pl.ds(r0, tr)
        gate = _dot(h_ref[0, rows, :], w_ref[:, GROUP_W:])
        o_ref[0, rows, :] = ((hs_s[0, rows, :] + hs_s[1, rows, :]) * _silu(gate)).astype(BF16)
        return carry

    lax.fori_loop(0, t_ // tr, epilogue, 0)


def _lru(h3, w, cw, cb, gw, gb, lam):
    b_, t_, _ = h3.shape
    return pl.pallas_call(
        functools.partial(_lru_kernel, t_=t_, tr=min(256, t_)),
        grid=(b_,),
        in_specs=[pl.BlockSpec((1, t_, D_MODEL), lambda b: (b, 0, 0)), _full(w), _full(cw), _full(cb),
                  _full(gw), _full(gb), _full(lam)],
        out_specs=pl.BlockSpec((1, t_, GROUP_W), lambda b: (b, 0, 0)),
        out_shape=jax.ShapeDtypeStruct((b_, t_, GROUP_W), BF16),
        scratch_shapes=[pltpu.VMEM((t_ + 2 * HALO, GROUP_W), F32),
                        pltpu.VMEM((2, t_, GROUP_W), F32), pltpu.VMEM((2, t_, GROUP_W), F32),
                        pltpu.VMEM((2, t_, GROUP_W), F32)],
        compiler_params=_params(),
        name="lru",
    )(h3, w, cw, cb, gw, gb, lam)


def _rwkv_tables():
    bd = _head_blockdiag()
    t = np.arange(CHUNK)
    tri = np.stack([(t[:, None] >= t[None, :]), (t[:, None] <= t[None, :])]).astype(np.float32)
    strict = np.stack([(t[:, None] > t[None, :]), (t[:, None] < t[None, :])]).astype(np.float32)
    incl = tri.copy()
    eye = np.eye(CHUNK, dtype=np.float32)
    wide = lambda m: np.tile(m, (1,) * (m.ndim - 1) + (N_HEADS_G,))
    return dict(bd=bd, tri=tri, strict=wide(strict), incl=wide(incl), eye=wide(eye))


def _rwkv_kernel(h_ref, w_ref, sh_ref, w0_ref, wup_ref, a0_ref, aup_ref, kk_ref, ka_ref, rk_ref,
                 lng_ref, lnb_ref, bd_ref, tri_ref, strict_ref, incl_ref, eye_ref, o_ref,
                 zs_s, r_s, kk_s, v_s, lw_s, qd_s, kd_s, y_s, st_s, *, t_, tr, prec):
    bd = bd_ref[...]
    g = GROUP_W
    sw = RWKV_SHIFT_W

    zhalo = jnp.zeros((HALO, sw), F32)
    zs_s[0:HALO, :] = zhalo
    zs_s[HALO + t_:, :] = zhalo

    def proj(i, carry):
        r0 = pl.multiple_of(i * tr, tr)
        zs_s[pl.ds(pl.multiple_of(HALO + r0, HALO), tr), :] = _dot(h_ref[0, pl.ds(r0, tr), :], w_ref[:, 0:sw])
        return carry

    lax.fori_loop(0, t_ // tr, proj, 0)

    def prep(i, carry):
        r0 = pl.multiple_of(i * tr, tr)
        rows = pl.ds(r0, tr)

        def mixed(c0, c1):
            win = zs_s[pl.ds(r0, tr + 2 * HALO), c0:c1]
            x = win[HALO:HALO + tr, :]
            prev = win[HALO - 1:HALO - 1 + tr, :]
            nxt = win[HALO + 1:HALO + 1 + tr, :]
            return x + sh_ref[0:1, c0:c1] * (prev - x) + sh_ref[1:2, c0:c1] * (nxt - x)

        r_s[rows, :] = mixed(0, g)
        v = mixed(2 * g, 3 * g)
        v_s[rows, :] = v
        k = mixed(g, 2 * g)
        lo = mixed(3 * g, sw)
        kk = k * kk_ref[...]
        ssq = _dot(kk * kk, bd, HI)
        kk = kk / jnp.maximum(jnp.sqrt(ssq), 1e-12)
        kk_s[rows, :] = kk
        dw = _dot(jnp.tanh(lo).astype(BF16), wup_ref[...])
        da = _dot(lo.astype(BF16), aup_ref[...])
        for d in range(2):
            lw_s[d, rows, :] = -EXP_M05 * _sigmoid(w0_ref[d:d + 1, :] + dw[:, d * g:(d + 1) * g])
            a = _sigmoid(a0_ref[d:d + 1, :] + da[:, d * g:(d + 1) * g])
            kd_s[d, rows, :] = k * (1.0 + (a - 1.0) * ka_ref[...])
            qd_s[d, rows, :] = kk * a
        y_s[rows, :] = jnp.zeros((tr, g), F32)
        return carry

    lax.fori_loop(0, t_ // tr, prep, 0)
    st_s[...] = jnp.zeros((2, g, g), F32)

    nc = t_ // CHUNK
    rep = lambda x: jnp.concatenate([x] * N_HEADS_G, axis=0) * bd

    def chunk(i, carry):
        for d in range(2):
            c = i if d == 0 else nc - 1 - i
            r0 = pl.multiple_of(c * CHUNK, CHUNK)
            rows = pl.ds(r0, CHUNK)
            lw = lw_s[d, rows, :]
            lg = _dot(tri_ref[d], lw, HI)
            tot = lg[CHUNK - 1:CHUNK, :] if d == 0 else lg[0:1, :]
            g_in = jnp.exp(lg)
            g_inv = jnp.exp(-lg)
            g_end = jnp.exp(tot - lg)
            kkc = kk_s[rows, :]
            pt = kkc * jnp.exp(lg - lw)
            qdc = qd_s[d, rows, :]
            kdc = kd_s[d, rows, :]
            qt = qdc * g_inv
            kt = kdc * g_inv
            rt = r_s[rows, :] * g_in
            vc = v_s[rows, :]
            strict = strict_ref[d]
            incl = incl_ref[d]
            qbd = rep(qt)
            kbd = rep(kt)
            a_pq = _dot_nt(pt, qbd, prec) * strict
            a_pk = _dot_nt(pt, kbd, prec) * strict
            a_rq = _dot_nt(rt, qbd, prec) * incl
            a_rk = _dot_nt(rt, kbd, prec) * incl
            x = -a_pq
            tm = eye_ref[...] + x
            for _ in range(5):
                x = _dot(x, rep(x), prec)
                tm = tm + _dot(tm, rep(x), prec)
            wm = _dot(tm, rep(pt), prec)
            vbd = rep(vc)
            u0 = _dot(tm, rep(_dot(a_pk, vbd, prec)), prec)
            s = st_s[d]
            u = _dot_nt(wm, s, prec) + u0
            y = _dot_nt(rt, s, prec) + _dot(a_rk, vbd, prec) - _dot(a_rq, rep(u), prec)
            y_s[rows, :] = y_s[rows, :] + y
            upd = _dot_tn(vc, kdc * g_end, prec) - _dot_tn(u, qdc * g_end, prec)
            st_s[d] = s * jnp.exp(tot) + upd * bd
        return carry

    lax.fori_loop(0, nc, chunk, 0)

    def epilogue(i, carry):
        r0 = pl.multiple_of(i * tr, tr)
        rows = pl.ds(r0, tr)
        y = y_s[rows, :]
        mu = _dot(y, bd, HI) * (1.0 / HEAD_DIM)
        yc = y - mu
        var = _dot(yc * yc, bd, HI) * (1.0 / HEAD_DIM)
        ksum = kd_s[0, rows, :] + kd_s[1, rows, :]
        bonus = _dot(r_s[rows, :] * ksum * rk_ref[...], bd, HI) * v_s[rows, :]
        yn = yc * lax.rsqrt(var + GN_EPS) * lng_ref[...] + lnb_ref[...] + bonus
        gate = _dot(h_ref[0, rows, :], w_ref[:, sw:])
        o_ref[0, rows, :] = (yn * _silu(gate)).astype(BF16)
        return carry

    lax.fori_loop(0, t_ // tr, epilogue, 0)


def _rwkv(h3, w, p, tabs, prec):
    b_, t_, _ = h3.shape
    args = [h3, w, p["shift"], p["w0"], p["wup"], p["a0"], p["aup"], p["kk"], p["ka"], p["rk"], p["lng"],
            p["lnb"], tabs["bd"], tabs["tri"], tabs["strict"], tabs["incl"], tabs["eye"]]
    big = lambda: pltpu.VMEM((t_, GROUP_W), F32)
    big2 = lambda: pltpu.VMEM((2, t_, GROUP_W), F32)
    return pl.pallas_call(
        functools.partial(_rwkv_kernel, t_=t_, tr=min(128, t_), prec=prec),
        grid=(b_,),
        in_specs=[pl.BlockSpec((1, t_, D_MODEL), lambda b: (b, 0, 0))] + [_full(a) for a in args[1:]],
        out_specs=pl.BlockSpec((1, t_, GROUP_W), lambda b: (b, 0, 0)),
        out_shape=jax.ShapeDtypeStruct((b_, t_, GROUP_W), BF16),
        scratch_shapes=[pltpu.VMEM((t_ + 2 * HALO, RWKV_SHIFT_W), F32),
                        big(), big(), big(), big2(), big2(), big2(), big(),
                        pltpu.VMEM((2, GROUP_W, GROUP_W), F32)],
        compiler_params=_params(),
        name="rwkv",
    )(*args)


def _prep_layer(l, P):
    w_in = P["w_in"][l].astype(BF16)
    c0, c1, c2 = A_W, A_W + B_W, A_W + B_W + C_W
    pad = jnp.zeros((2, RWKV_DECAY_RANK, GROUP_W), F32)
    wup = jnp.concatenate([P["rwkv_w_up"][l], pad], axis=1)
    aup = jnp.concatenate([pad, P["rwkv_a_up"][l]], axis=1)
    cat2 = lambda m: jnp.concatenate([m[0], m[1]], axis=1).astype(BF16)
    gw = P["lru_gate_w"][l]
    blocks = []
    for d in range(2):
        for kk in range(2):
            blocks.append(jax.scipy.linalg.block_diag(*[gw[d, kk, n] for n in range(LRU_BLOCKS)]))
    row = lambda a: a.reshape(1, -1)
    return dict(
        wA=w_in[:, :c0], wB=w_in[:, c0:c1], wC=w_in[:, c1:c2], wD=w_in[:, c2:],
        w_out=P["w_out"][l].astype(BF16),
        rwkv=dict(shift=P["rwkv_shift"][l], w0=P["rwkv_w0"][l], wup=cat2(wup), a0=P["rwkv_a0"][l],
                  aup=cat2(aup), kk=row(P["rwkv_k_k"][l]), ka=row(P["rwkv_k_a"][l]),
                  rk=row(P["rwkv_r_k"][l]), lng=row(P["rwkv_ln_g"][l]), lnb=row(P["rwkv_ln_b"][l])),
        qw=jnp.tile(P["attn_q_norm"][l], 2).reshape(1, -1), kw=jnp.tile(P["attn_k_norm"][l], 2).reshape(1, -1),
        cw=P["lru_conv_w"][l], cb=row(P["lru_conv_b"][l]),
        gw=jnp.concatenate(blocks, axis=1).astype(BF16), gb=P["lru_gate_b"][l].reshape(1, -1),
        lam=P["lru_lambda"][l], sink=P["swa_sink"][l],
    )


def _tables(t_):
    cos, sin = _rope_tables(t_)
    qi = np.arange(Q_BLOCK)[:, None]
    ki = np.arange(3 * Q_BLOCK)[None, :]
    dist = np.abs(ki - WINDOW - qi).astype(np.float32)
    slopes = np.exp2(-8.0 * np.arange(1, N_HEADS_G + 1, dtype=np.float32) / N_HEADS_G).astype(np.float32)
    wbias = (-slopes[:, None, None] * dist[None]).astype(np.float32)
    tabs = dict(cos=cos, sin=sin, seg128=_head_blockdiag()[:128, :128], sel=_kv_select(), wbias=wbias)
    tabs.update(_rwkv_tables())
    out = {k: jnp.asarray(v) for k, v in tabs.items()}
    out["sel"] = out["sel"].astype(BF16)
    return out


def _trunk(x, P, layers, prec=HI):
    b_, t_, _ = x.shape
    tabs = _tables(t_)
    x2 = x.reshape(b_ * t_, D_MODEL)
    h2 = _norm(x2, P["norm_g"][0].reshape(1, -1))
    for l in range(DEPTH):
        lp = layers[l]
        h3 = h2.reshape(b_, t_, D_MODEL)
        oa = _rwkv(h3, lp["wA"], lp["rwkv"], tabs, prec)
        ob = _gattn(h3, lp["wB"], lp["qw"], lp["kw"], tabs)
        oc = _lru(h3, lp["wC"], lp["cw"], lp["cb"], lp["gw"], lp["gb"], lp["lam"])
        od = _wattn(h3, lp["wD"], lp["sink"], tabs)
        flat = lambda o: o.reshape(b_ * t_, GROUP_W)
        final = l == DEPTH - 1
        g_next = (P["final_g"] if final else P["norm_g"][l + 1]).reshape(1, -1)
        outs = _out_proj(flat(oa), flat(ob), flat(oc), flat(od), lp["w_out"], x2, g_next, final)
        if final:
            x2 = outs[0]
        else:
            x2, h2 = outs
    return x2.reshape(b_, t_, D_MODEL)


def kernel(x_prompt, x_sample, norm_g, w_in, w_out, rwkv_shift, rwkv_w0, rwkv_w_up, rwkv_a0, rwkv_a_up,
           rwkv_k_k, rwkv_k_a, rwkv_r_k, rwkv_ln_g, rwkv_ln_b, attn_q_norm, attn_k_norm, lru_conv_w,
           lru_conv_b, lru_gate_w, lru_gate_b, lru_lambda, swa_sink, final_g):
    P = dict(norm_g=norm_g, w_in=w_in, w_out=w_out, rwkv_shift=rwkv_shift, rwkv_w0=rwkv_w0,
             rwkv_w_up=rwkv_w_up, rwkv_a0=rwkv_a0, rwkv_a_up=rwkv_a_up, rwkv_k_k=rwkv_k_k,
             rwkv_k_a=rwkv_k_a, rwkv_r_k=rwkv_r_k, rwkv_ln_g=rwkv_ln_g, rwkv_ln_b=rwkv_ln_b,
             attn_q_norm=attn_q_norm, attn_k_norm=attn_k_norm, lru_conv_w=lru_conv_w,
             lru_conv_b=lru_conv_b, lru_gate_w=lru_gate_w, lru_gate_b=lru_gate_b,
             lru_lambda=lru_lambda, swa_sink=swa_sink, final_g=final_g)
    layers = [_prep_layer(l, P) for l in range(DEPTH)]
    return _trunk(x_prompt, P, layers), _trunk(x_sample, P, layers)
```

```python
import functools
import math

import jax
import jax.numpy as jnp
import numpy as np
from jax import lax
from jax.experimental import pallas as pl
from jax.experimental.pallas import tpu as pltpu

D_MODEL = 1024
DEPTH = 4
GRID_W = 64
HEAD_DIM = 64
GROUP_W = 256
N_HEADS_G = 4
N_KV = 2
REP = 2
KV_W = 128
RWKV_DECAY_RANK = 64
RWKV_SHIFT_W = 896
LRU_C = 8.0
LRU_BLOCKS = 4
LRU_BLK = 64
CONV_W = 4
CONV_LEFT = 2
Q_BLOCK = 128
WINDOW = 128
ROPE_THETA = 10000.0
NORM_EPS = 1e-6
GN_EPS = 64e-5
NEG = -1e30
A_W = 1152
B_W = 768
C_W = 512
D_W = 768

CHUNK = 64
ROW_TILE = 512
HALO = 8
V7X_VMEM_LIMIT = 56 * 1024 * 1024
F32 = jnp.float32
BF16 = jnp.bfloat16
HI = lax.Precision.HIGHEST
EXP_M05 = math.exp(-0.5)


def _dot(a, b, prec=None):
    return jnp.dot(a, b, preferred_element_type=F32, precision=prec)


def _dot_nt(a, b, prec=None):
    return lax.dot_general(a, b, (((1,), (1,)), ((), ())), preferred_element_type=F32, precision=prec)


def _dot_tn(a, b, prec=None):
    return lax.dot_general(a, b, (((0,), (0,)), ((), ())), preferred_element_type=F32, precision=prec)


def _sigmoid(x):
    return 1.0 / (1.0 + jnp.exp(-x))


def _silu(x):
    return x * _sigmoid(x)


def _rms(x, g):
    return x * lax.rsqrt(jnp.mean(x * x, axis=-1, keepdims=True) + NORM_EPS) * g


def _params(n_axes=1):
    return pltpu.CompilerParams(dimension_semantics=("arbitrary",) * n_axes,
                                vmem_limit_bytes=V7X_VMEM_LIMIT)


def _full(a):
    nd = a.ndim
    return pl.BlockSpec(a.shape, lambda *_: (0,) * nd)


def _norm_kernel(x_ref, g_ref, h_ref):
    h_ref[...] = _rms(x_ref[...], g_ref[...]).astype(BF16)


def _norm(x2, g):
    m = x2.shape[0]
    tm = min(ROW_TILE, m)
    return pl.pallas_call(
        _norm_kernel,
        grid=(m // tm,),
        in_specs=[pl.BlockSpec((tm, D_MODEL), lambda i: (i, 0)), _full(g)],
        out_specs=pl.BlockSpec((tm, D_MODEL), lambda i: (i, 0)),
        out_shape=jax.ShapeDtypeStruct((m, D_MODEL), BF16),
        compiler_params=_params(),
        name="norm",
    )(x2, g)


def _out_kernel(oa_ref, ob_ref, oc_ref, od_ref, w_ref, x_ref, g_ref, *out_refs, final):
    acc = _dot(oa_ref[...], w_ref[0:GROUP_W, :])
    acc += _dot(ob_ref[...], w_ref[GROUP_W:2 * GROUP_W, :])
    acc += _dot(oc_ref[...], w_ref[2 * GROUP_W:3 * GROUP_W, :])
    acc += _dot(od_ref[...], w_ref[3 * GROUP_W:4 * GROUP_W, :])
    xn = x_ref[...] + acc
    if final:
        out_refs[0][...] = _rms(xn, g_ref[...])
    else:
        out_refs[0][...] = xn
        out_refs[1][...] = _rms(xn, g_ref[...]).astype(BF16)


def _out_proj(oa, ob, oc, od, w, x2, g, final):
    m = x2.shape[0]
    tm = min(ROW_TILE, m)
    ospec = pl.BlockSpec((tm, GROUP_W), lambda i: (i, 0))
    xspec = pl.BlockSpec((tm, D_MODEL), lambda i: (i, 0))
    if final:
        out_shape = (jax.ShapeDtypeStruct((m, D_MODEL), F32),)
        out_specs = (xspec,)
    else:
        out_shape = (jax.ShapeDtypeStruct((m, D_MODEL), F32), jax.ShapeDtypeStruct((m, D_MODEL), BF16))
        out_specs = (xspec, xspec)
    return pl.pallas_call(
        functools.partial(_out_kernel, final=final),
        grid=(m // tm,),
        in_specs=[ospec, ospec, ospec, ospec, _full(w), xspec, _full(g)],
        out_specs=out_specs,
        out_shape=out_shape,
        compiler_params=_params(),
        name="out_proj",
    )(oa, ob, oc, od, w, x2, g)


def _head_blockdiag():
    i = np.arange(GROUP_W)
    return (i[:, None] // HEAD_DIM == i[None, :] // HEAD_DIM).astype(np.float32)


def _rope_tables(t_):
    rows = t_ // GRID_W
    row = np.repeat(np.arange(rows), GRID_W).astype(np.float32)
    col = np.tile(np.arange(GRID_W), rows).astype(np.float32)
    half = HEAD_DIM // 2
    inv = (ROPE_THETA ** (-np.arange(0, half, 2, dtype=np.float32) / half)).astype(np.float32)
    ang_r = row[:, None] * inv
    ang_c = col[:, None] * inv
    cos = np.concatenate([np.cos(ang_r), np.cos(ang_r), np.cos(ang_c), np.cos(ang_c)], -1)
    sin = np.concatenate([-np.sin(ang_r), np.sin(ang_r), -np.sin(ang_c), np.sin(ang_c)], -1)
    return (np.tile(cos, (1, 2)).astype(np.float32), np.tile(sin, (1, 2)).astype(np.float32))


def _kv_select():
    sel = np.zeros((N_HEADS_G, KV_W, GROUP_W), np.float32)
    for h in range(N_HEADS_G):
        g = h // REP
        for d in range(HEAD_DIM):
            sel[h, g * HEAD_DIM + d, h * HEAD_DIM + d] = 1.0
    return sel


def _swap16(x):
    n = x.shape[-1]
    up = pltpu.roll(x, n - 16, axis=1)
    dn = pltpu.roll(x, 16, axis=1)
    lane = lax.broadcasted_iota(jnp.int32, x.shape, 1)
    return jnp.where((lane % 32) < 16, up, dn)


def _gattn_kernel(h_ref, w_ref, qw_ref, kw_ref, cos_ref, sin_ref, seg_ref, sel_ref, o_ref,
                  q_s, kw_s, vw_s, *, t_, tq, tr):
    seg = seg_ref[...]

    def prologue(i, carry):
        r0 = pl.multiple_of(i * tr, tr)
        rows = pl.ds(r0, tr)
        h = h_ref[0, rows, :]
        cos = cos_ref[rows, :]
        sin = sin_ref[rows, :]

        def normrope(z, wgt):
            ms = _dot(z * z, seg, HI) * (1.0 / HEAD_DIM)
            z = z * lax.rsqrt(ms + NORM_EPS) * wgt
            return z * cos + _swap16(z) * sin

        for half in range(2):
            zq = _dot(h, w_ref[:, half * 128:(half + 1) * 128])
            qr = normrope(zq, qw_ref[...]) * (HEAD_DIM ** -0.5)
            q_s[rows, half * 128:(half + 1) * 128] = qr.astype(BF16)
        zk = _dot(h, w_ref[:, GROUP_W:GROUP_W + KV_W])
        kr = normrope(zk, kw_ref[...]).astype(BF16)
        zv = _dot(h, w_ref[:, GROUP_W + KV_W:GROUP_W + 2 * KV_W]).astype(BF16)
        for hd in range(N_HEADS_G):
            kw_s[hd, rows, :] = _dot(kr, sel_ref[hd]).astype(BF16)
            vw_s[hd, rows, :] = _dot(zv, sel_ref[hd]).astype(BF16)
        return carry

    lax.fori_loop(0, t_ // tr, prologue, 0)

    def qblock(i, carry):
        r0 = pl.multiple_of(i * tq, tq)
        q = q_s[pl.ds(r0, tq), :]
        acc = jnp.zeros((tq, GROUP_W), F32)
        for hd in range(N_HEADS_G):
            s = _dot_nt(q, kw_s[hd])
            m = jnp.max(s, axis=-1, keepdims=True)
            p = jnp.exp(s - m)
            l = jnp.sum(p, axis=-1, keepdims=True)
            acc += _dot(p.astype(BF16), vw_s[hd]) * (1.0 / l)
        gate = _dot(h_ref[0, pl.ds(r0, tq), :], w_ref[:, GROUP_W + 2 * KV_W:])
        o_ref[0, pl.ds(r0, tq), :] = (acc * _silu(gate)).astype(BF16)
        return carry

    lax.fori_loop(0, t_ // tq, qblock, 0)


def _gattn(h3, w, qw, kw, tabs):
    b_, t_, _ = h3.shape
    tq = min(256, t_)
    cos, sin, seg, sel = tabs["cos"], tabs["sin"], tabs["seg128"], tabs["sel"]
    return pl.pallas_call(
        functools.partial(_gattn_kernel, t_=t_, tq=tq, tr=min(ROW_TILE, t_)),
        grid=(b_,),
        in_specs=[pl.BlockSpec((1, t_, D_MODEL), lambda b: (b, 0, 0)), _full(w), _full(qw), _full(kw),
                  _full(cos), _full(sin), _full(seg), _full(sel)],
        out_specs=pl.BlockSpec((1, t_, GROUP_W), lambda b: (b, 0, 0)),
        out_shape=jax.ShapeDtypeStruct((b_, t_, GROUP_W), BF16),
        scratch_shapes=[pltpu.VMEM((t_, GROUP_W), BF16),
                        pltpu.VMEM((N_HEADS_G, t_, GROUP_W), BF16),
                        pltpu.VMEM((N_HEADS_G, t_, GROUP_W), BF16)],
        compiler_params=_params(),
        name="gattn",
    )(h3, w, qw, kw, cos, sin, seg, sel)


def _wattn_kernel(sink_ref, h_ref, w_ref, bias_ref, sel_ref, o_ref, q_s, kp_s, vp_s, *, t_, tr):
    nb = t_ // Q_BLOCK
    zpad = jnp.zeros((WINDOW, GROUP_W), BF16)
    for hd in range(N_HEADS_G):
        kp_s[hd, 0:WINDOW, :] = zpad
        kp_s[hd, WINDOW + t_:, :] = zpad
        vp_s[hd, 0:WINDOW, :] = zpad
        vp_s[hd, WINDOW + t_:, :] = zpad

    def prologue(i, carry):
        r0 = pl.multiple_of(i * tr, tr)
        rows = pl.ds(r0, tr)
        prow = pl.ds(pl.multiple_of(WINDOW + r0, WINDOW), tr)
        h = h_ref[0, rows, :]
        q_s[rows, :] = (_dot(h, w_ref[:, 0:GROUP_W]) * (HEAD_DIM ** -0.5)).astype(BF16)
        zk = _dot(h, w_ref[:, GROUP_W:GROUP_W + KV_W]).astype(BF16)
        zv = _dot(h, w_ref[:, GROUP_W + KV_W:GROUP_W + 2 * KV_W]).astype(BF16)
        for hd in range(N_HEADS_G):
            kp_s[hd, prow, :] = _dot(zk, sel_ref[hd]).astype(BF16)
            vp_s[hd, prow, :] = _dot(zv, sel_ref[hd]).astype(BF16)
        return carry

    lax.fori_loop(0, t_ // tr, prologue, 0)

    kw = 3 * Q_BLOCK
    col = lax.broadcasted_iota(jnp.int32, (Q_BLOCK, kw), 1)
    row = lax.broadcasted_iota(jnp.int32, (Q_BLOCK, kw), 0)
    band = jnp.abs(col - WINDOW - row) <= WINDOW

    def qblock(i, carry):
        r0 = pl.multiple_of(i * Q_BLOCK, Q_BLOCK)
        q = q_s[pl.ds(r0, Q_BLOCK), :]
        kpos = col + (i - 1) * Q_BLOCK
        valid = band & (kpos >= 0) & (kpos < t_)
        acc = jnp.zeros((Q_BLOCK, GROUP_W), F32)
        for hd in range(N_HEADS_G):
            s = _dot_nt(q, kp_s[hd, pl.ds(r0, kw), :])
            s = jnp.where(valid, s + bias_ref[hd], NEG)
            sk = sink_ref[hd]
            m = jnp.maximum(jnp.max(s, axis=-1, keepdims=True), sk)
            p = jnp.exp(s - m)
            den = jnp.sum(p, axis=-1, keepdims=True) + jnp.exp(sk - m)
            acc += _dot(p.astype(BF16), vp_s[hd, pl.ds(r0, kw), :]) * (1.0 / den)
        gate = _dot(h_ref[0, pl.ds(r0, Q_BLOCK), :], w_ref[:, GROUP_W + 2 * KV_W:])
        o_ref[0, pl.ds(r0, Q_BLOCK), :] = (acc * _silu(gate)).astype(BF16)
        return carry

    lax.fori_loop(0, nb, qblock, 0)


def _wattn(h3, w, sink, tabs):
    b_, t_, _ = h3.shape
    bias, sel = tabs["wbias"], tabs["sel"]
    return pl.pallas_call(
        functools.partial(_wattn_kernel, t_=t_, tr=min(ROW_TILE, t_)),
        grid=(b_,),
        in_specs=[pl.BlockSpec(memory_space=pltpu.SMEM),
                  pl.BlockSpec((1, t_, D_MODEL), lambda b: (b, 0, 0)), _full(w), _full(bias), _full(sel)],
        out_specs=pl.BlockSpec((1, t_, GROUP_W), lambda b: (b, 0, 0)),
        out_shape=jax.ShapeDtypeStruct((b_, t_, GROUP_W), BF16),
        scratch_shapes=[pltpu.VMEM((t_, GROUP_W), BF16),
                        pltpu.VMEM((N_HEADS_G, t_ + 2 * WINDOW, GROUP_W), BF16),
                        pltpu.VMEM((N_HEADS_G, t_ + 2 * WINDOW, GROUP_W), BF16)],
        compiler_params=_params(),
        name="wattn",
    )(sink, h3, w, bias, sel)


def _tile_scan(a, b, rev):
    sub = lax.broadcasted_iota(jnp.int32, a.shape, 0)
    for s in (1, 2, 4):
        if rev:
            a_sh = pltpu.roll(a, 8 - s, axis=0)
            b_sh = pltpu.roll(b, 8 - s, axis=0)
            ok = sub < 8 - s
        else:
            a_sh = pltpu.roll(a, s, axis=0)
            b_sh = pltpu.roll(b, s, axis=0)
            ok = sub >= s
        b = jnp.where(ok, a * b_sh + b, b)
        a = jnp.where(ok, a * a_sh, a)
    return a, b


def _lru_kernel(h_ref, w_ref, cw_ref, cb_ref, gw_ref, gb_ref, lam_ref, o_ref, xp_s, a_s, b_s, hs_s,
                *, t_, tr):
    zhalo = jnp.zeros((HALO, GROUP_W), F32)
    xp_s[0:HALO, :] = zhalo
    xp_s[HALO + t_:, :] = zhalo

    def proj(i, carry):
        r0 = pl.multiple_of(i * tr, tr)
        xp_s[pl.ds(pl.multiple_of(HALO + r0, HALO), tr), :] = _dot(h_ref[0, pl.ds(r0, tr), :], w_ref[:, 0:GROUP_W])
        return carry

    lax.fori_loop(0, t_ // tr, proj, 0)

    lam = lam_ref[...]
    sp = jnp.maximum(-lam, 0.0) + jnp.log(1.0 + jnp.exp(-jnp.abs(lam)))

    def gates(i, carry):
        r0 = pl.multiple_of(i * tr, tr)
        rows = pl.ds(r0, tr)
        win = xp_s[pl.ds(r0, tr + 2 * HALO), :]
        xc = cb_ref[...]
        for j in range(CONV_W):
            st = HALO + j - CONV_LEFT
            xc = xc + cw_ref[j:j + 1, :] * win[st:st + tr, :]
        g = _dot(xc.astype(BF16), gw_ref[...]) + gb_ref[...]
        for d in range(2):
            r = _sigmoid(g[:, (2 * d) * GROUP_W:(2 * d + 1) * GROUP_W])
            ig = _sigmoid(g[:, (2 * d + 1) * GROUP_W:(2 * d + 2) * GROUP_W])
            a = jnp.exp(-LRU_C * r * sp[d:d + 1, :])
            a_s[d, rows, :] = a
            b_s[d, rows, :] = jnp.sqrt(1.0 - a * a) * (ig * xc)
        return carry

    lax.fori_loop(0, t_ // tr, gates, 0)

    nt = t_ // 8

    def tile(i, carry):
        cf, cr = carry
        rf = pl.multiple_of(i * 8, 8)
        rr = pl.multiple_of((nt - 1 - i) * 8, 8)
        af, bf = _tile_scan(a_s[0, pl.ds(rf, 8), :], b_s[0, pl.ds(rf, 8), :], False)
        ar, br = _tile_scan(a_s[1, pl.ds(rr, 8), :], b_s[1, pl.ds(rr, 8), :], True)
        hf = bf + af * cf
        hr = br + ar * cr
        hs_s[0, pl.ds(rf, 8), :] = hf
        hs_s[1, pl.ds(rr, 8), :] = hr
        return hf[7:8, :], hr[0:1, :]

    z = jnp.zeros((1, GROUP_W), F32)
    lax.fori_loop(0, nt, tile, (z, z))

    def epilogue(i, carry):
        r0 = pl.multiple_of(i * tr, tr)
        rows = pl.ds(r0, tr)
        gate = _dot(h_ref[0, rows, :], w_ref[:, GROUP_W:])
        o_ref[0, rows, :] = ((hs_s[0, rows, :] + hs_s[1, rows, :]) * _silu(gate)).astype(BF16)
        return carry

    lax.fori_loop(0, t_ // tr, epilogue, 0)


def _lru(h3, w, cw, cb, gw, gb, lam):
    b_, t_, _ = h3.shape
    return pl.pallas_call(
        functools.partial(_lru_kernel, t_=t_, tr=min(256, t_)),
        grid=(b_,),
        in_specs=[pl.BlockSpec((1, t_, D_MODEL), lambda b: (b, 0, 0)), _full(w), _full(cw), _full(cb),
                  _full(gw), _full(gb), _full(lam)],
        out_specs=pl.BlockSpec((1, t_, GROUP_W), lambda b: (b, 0, 0)),
        out_shape=jax.ShapeDtypeStruct((b_, t_, GROUP_W), BF16),
        scratch_shapes=[pltpu.VMEM((t_ + 2 * HALO, GROUP_W), F32),
                        pltpu.VMEM((2, t_, GROUP_W), F32), pltpu.VMEM((2, t_, GROUP_W), F32),
                        pltpu.VMEM((2, t_, GROUP_W), F32)],
        compiler_params=_params(),
        name="lru",
    )(h3, w, cw, cb, gw, gb, lam)


def _rwkv_tables():
    bd = _head_blockdiag()
    t = np.arange(CHUNK)
    tri = np.stack([(t[:, None] >= t[None, :]), (t[:, None] <= t[None, :])]).astype(np.float32)
    strict = np.stack([(t[:, None] > t[None, :]), (t[:, None] < t[None, :])]).astype(np.float32)
    incl = tri.copy()
    eye = np.eye(CHUNK, dtype=np.float32)
    wide = lambda m: np.tile(m, (1,) * (m.ndim - 1) + (N_HEADS_G,))
    return dict(bd=bd, bdb=bd, tri2=np.concatenate([tri, tri], axis=-1), strict=wide(strict), incl=wide(incl),
                eye=wide(eye))


def _rwkv_kernel(h_ref, w_ref, sh_ref, w0_ref, wup_ref, a0_ref, aup_ref, kk_ref, ka_ref, rk_ref,
                 lng_ref, lnb_ref, bd_ref, bdb_ref, tri_ref, strict_ref, incl_ref, eye_ref, o_ref,
                 zs_s, r_s, kk_s, v_s, lw_s, qd_s, kd_s, y_s, st_s, *, t_, tr):
    bd = bd_ref[...]
    g = GROUP_W
    sw = RWKV_SHIFT_W

    zhalo = jnp.zeros((HALO, sw), F32)
    zs_s[0:HALO, :] = zhalo
    zs_s[HALO + t_:, :] = zhalo

    def proj(i, carry):
        r0 = pl.multiple_of(i * tr, tr)
        zs_s[pl.ds(pl.multiple_of(HALO + r0, HALO), tr), :] = _dot(h_ref[0, pl.ds(r0, tr), :], w_ref[:, 0:sw])
        return carry

    lax.fori_loop(0, t_ // tr, proj, 0)

    def prep(i, carry):
        r0 = pl.multiple_of(i * tr, tr)
        rows = pl.ds(r0, tr)

        def mixed(c0, c1):
            win = zs_s[pl.ds(r0, tr + 2 * HALO), c0:c1]
            x = win[HALO:HALO + tr, :]
            prev = win[HALO - 1:HALO - 1 + tr, :]
            nxt = win[HALO + 1:HALO + 1 + tr, :]
            return x + sh_ref[0:1, c0:c1] * (prev - x) + sh_ref[1:2, c0:c1] * (nxt - x)

        r_s[rows, :] = mixed(0, g)
        v = mixed(2 * g, 3 * g)
        v_s[rows, :] = v
        k = mixed(g, 2 * g)
        lo = mixed(3 * g, sw)
        kk = k * kk_ref[...]
        ssq = _dot(kk * kk, bd, HI)
        kk = kk / jnp.maximum(jnp.sqrt(ssq), 1e-12)
        kk_s[rows, :] = kk
        dw = _dot(jnp.tanh(lo).astype(BF16), wup_ref[...])
        da = _dot(lo.astype(BF16), aup_ref[...])
        for d in range(2):
            lw_s[d, rows, :] = -EXP_M05 * _sigmoid(w0_ref[d:d + 1, :] + dw[:, d * g:(d + 1) * g])
            a = _sigmoid(a0_ref[d:d + 1, :] + da[:, d * g:(d + 1) * g])
            kd_s[d, rows, :] = k * (1.0 + (a - 1.0) * ka_ref[...])
            qd_s[d, rows, :] = kk * a
        y_s[rows, :] = jnp.zeros((tr, g), F32)
        return carry

    lax.fori_loop(0, t_ // tr, prep, 0)
    st_s[...] = jnp.zeros((2, g, g), F32)

    nc = t_ // CHUNK
    bdb = bdb_ref[...]
    cc = CHUNK
    bf = lambda x: x.astype(BF16)
    rep = lambda x: jnp.concatenate([bf(x)] * N_HEADS_G, axis=0) * bdb
    stack = lambda a, b: bf(jnp.concatenate([a, b], axis=0))

    def chunk(i, carry):
        for d in range(2):
            c = i if d == 0 else nc - 1 - i
            r0 = pl.multiple_of(c * CHUNK, CHUNK)
            rows = pl.ds(r0, CHUNK)
            lw = lw_s[d, rows, :]
            lw_hi = bf(lw)
            lw_lo = bf(lw - lw_hi.astype(F32))
            lg = _dot(tri_ref[d], jnp.concatenate([lw_hi, lw_lo], axis=0))
            tot = lg[cc - 1:cc, :] if d == 0 else lg[0:1, :]
            g_in = jnp.exp(lg)
            g_inv = jnp.exp(-lg)
            g_end = jnp.exp(tot - lg)
            pt = kk_s[rows, :] * jnp.exp(lg - lw)
            qdc = qd_s[d, rows, :]
            kdc = kd_s[d, rows, :]
            rt = r_s[rows, :] * g_in
            vc = v_s[rows, :]
            strict = strict_ref[d]
            incl = incl_ref[d]
            pr = stack(pt, rt)
            a_q = _dot_nt(pr, rep(qdc * g_inv))
            a_k = _dot_nt(pr, rep(kdc * g_inv))
            a_rq = a_q[cc:, :] * incl
            x = -(a_q[:cc, :] * strict)
            tm = eye_ref[...] + x
            for _ in range(5):
                x = _dot(bf(x), rep(x))
                tm = tm + _dot(bf(tm), rep(x))
            tmb = bf(tm)
            wm = _dot(tmb, rep(pt))
            av = _dot(stack(a_k[:cc, :] * strict, a_k[cc:, :] * incl), rep(vc))
            u0 = _dot(tmb, rep(av[:cc, :]))
            s = st_s[d]
            us = _dot_nt(stack(wm, rt), bf(s))
            u = us[:cc, :] + u0
            y = us[cc:, :] + av[cc:, :] - _dot(bf(a_rq), rep(u))
            y_s[rows, :] = y_s[rows, :] + y
            upd = _dot_tn(jnp.concatenate([vc, -u], axis=0), jnp.concatenate([kdc * g_end, qdc * g_end], axis=0))
            st_s[d] = s * jnp.exp(tot) + upd * bd
        return carry

    lax.fori_loop(0, nc, chunk, 0)

    def epilogue(i, carry):
        r0 = pl.multiple_of(i * tr, tr)
        rows = pl.ds(r0, tr)
        y = y_s[rows, :]
        mu = _dot(y, bd, HI) * (1.0 / HEAD_DIM)
        yc = y - mu
        var = _dot(yc * yc, bd, HI) * (1.0 / HEAD_DIM)
        ksum = kd_s[0, rows, :] + kd_s[1, rows, :]
        bonus = _dot(r_s[rows, :] * ksum * rk_ref[...], bd, HI) * v_s[rows, :]
        yn = yc * lax.rsqrt(var + GN_EPS) * lng_ref[...] + lnb_ref[...] + bonus
        gate = _dot(h_ref[0, rows, :], w_ref[:, sw:])
        o_ref[0, rows, :] = (yn * _silu(gate)).astype(BF16)
        return carry

    lax.fori_loop(0, t_ // tr, epilogue, 0)


def _rwkv(h3, w, p, tabs):
    b_, t_, _ = h3.shape
    args = [h3, w, p["shift"], p["w0"], p["wup"], p["a0"], p["aup"], p["kk"], p["ka"], p["rk"], p["lng"],
            p["lnb"], tabs["bd"], tabs["bdb"], tabs["tri2"], tabs["strict"], tabs["incl"], tabs["eye"]]
    big = lambda: pltpu.VMEM((t_, GROUP_W), F32)
    big2 = lambda: pltpu.VMEM((2, t_, GROUP_W), F32)
    return pl.pallas_call(
        functools.partial(_rwkv_kernel, t_=t_, tr=min(128, t_)),
        grid=(b_,),
        in_specs=[pl.BlockSpec((1, t_, D_MODEL), lambda b: (b, 0, 0))] + [_full(a) for a in args[1:]],
        out_specs=pl.BlockSpec((1, t_, GROUP_W), lambda b: (b, 0, 0)),
        out_shape=jax.ShapeDtypeStruct((b_, t_, GROUP_W), BF16),
        scratch_shapes=[pltpu.VMEM((t_ + 2 * HALO, RWKV_SHIFT_W), F32),
                        big(), big(), big(), big2(), big2(), big2(), big(),
                        pltpu.VMEM((2, GROUP_W, GROUP_W), F32)],
        compiler_params=_params(),
        name="rwkv",
    )(*args)


def _prep_layer(l, P):
    w_in = P["w_in"][l].astype(BF16)
    c0, c1, c2 = A_W, A_W + B_W, A_W + B_W + C_W
    pad = jnp.zeros((2, RWKV_DECAY_RANK, GROUP_W), F32)
    wup = jnp.concatenate([P["rwkv_w_up"][l], pad], axis=1)
    aup = jnp.concatenate([pad, P["rwkv_a_up"][l]], axis=1)
    cat2 = lambda m: jnp.concatenate([m[0], m[1]], axis=1).astype(BF16)
    gw = P["lru_gate_w"][l]
    blocks = []
    for d in range(2):
        for kk in range(2):
            blocks.append(jax.scipy.linalg.block_diag(*[gw[d, kk, n] for n in range(LRU_BLOCKS)]))
    row = lambda a: a.reshape(1, -1)
    return dict(
        wA=w_in[:, :c0], wB=w_in[:, c0:c1], wC=w_in[:, c1:c2], wD=w_in[:, c2:],
        w_out=P["w_out"][l].astype(BF16),
        rwkv=dict(shift=P["rwkv_shift"][l], w0=P["rwkv_w0"][l], wup=cat2(wup), a0=P["rwkv_a0"][l],
                  aup=cat2(aup), kk=row(P["rwkv_k_k"][l]), ka=row(P["rwkv_k_a"][l]),
                  rk=row(P["rwkv_r_k"][l]), lng=row(P["rwkv_ln_g"][l]), lnb=row(P["rwkv_ln_b"][l])),
        qw=jnp.tile(P["attn_q_norm"][l], 2).reshape(1, -1), kw=jnp.tile(P["attn_k_norm"][l], 2).reshape(1, -1),
        cw=P["lru_conv_w"][l], cb=row(P["lru_conv_b"][l]),
        gw=jnp.concatenate(blocks, axis=1).astype(BF16), gb=P["lru_gate_b"][l].reshape(1, -1),
        lam=P["lru_lambda"][l], sink=P["swa_sink"][l],
    )


def _tables(t_):
    cos, sin = _rope_tables(t_)
    qi = np.arange(Q_BLOCK)[:, None]
    ki = np.arange(3 * Q_BLOCK)[None, :]
    dist = np.abs(ki - WINDOW - qi).astype(np.float32)
    slopes = np.exp2(-8.0 * np.arange(1, N_HEADS_G + 1, dtype=np.float32) / N_HEADS_G).astype(np.float32)
    wbias = (-slopes[:, None, None] * dist[None]).astype(np.float32)
    tabs = dict(cos=cos, sin=sin, seg128=_head_blockdiag()[:128, :128], sel=_kv_select(), wbias=wbias)
    tabs.update(_rwkv_tables())
    out = {k: jnp.asarray(v) for k, v in tabs.items()}
    for k in ("sel", "bdb", "tri2"):
        out[k] = out[k].astype(BF16)
    return out


def _trunk(x, P, layers):
    b_, t_, _ = x.shape
    tabs = _tables(t_)
    x2 = x.reshape(b_ * t_, D_MODEL)
    h2 = _norm(x2, P["norm_g"][0].reshape(1, -1))
    for l in range(DEPTH):
        lp = layers[l]
        h3 = h2.reshape(b_, t_, D_MODEL)
        oa = _rwkv(h3, lp["wA"], lp["rwkv"], tabs)
        ob = _gattn(h3, lp["wB"], lp["qw"], lp["kw"], tabs)
        oc = _lru(h3, lp["wC"], lp["cw"], lp["cb"], lp["gw"], lp["gb"], lp["lam"])
        od = _wattn(h3, lp["wD"], lp["sink"], tabs)
        flat = lambda o: o.reshape(b_ * t_, GROUP_W)
        final = l == DEPTH - 1
        g_next = (P["final_g"] if final else P["norm_g"][l + 1]).reshape(1, -1)
        outs = _out_proj(flat(oa), flat(ob), flat(oc), flat(od), lp["w_out"], x2, g_next, final)
        if final:
            x2 = outs[0]
        else:
            x2, h2 = outs
    return x2.reshape(b_, t_, D_MODEL)


def kernel(x_prompt, x_sample, norm_g, w_in, w_out, rwkv_shift, rwkv_w0, rwkv_w_up, rwkv_a0, rwkv_a_up,
           rwkv_k_k, rwkv_k_a, rwkv_r_k, rwkv_ln_g, rwkv_ln_b, attn_q_norm, attn_k_norm, lru_conv_w,
           lru_conv_b, lru_gate_w, lru_gate_b, lru_lambda, swa_sink, final_g):
    P = dict(norm_g=norm_g, w_in=w_in, w_out=w_out, rwkv_shift=rwkv_shift, rwkv_w0=rwkv_w0,
             rwkv_w_up=rwkv_w_up, rwkv_a0=rwkv_a0, rwkv_a_up=rwkv_a_up, rwkv_k_k=rwkv_k_k,
             rwkv_k_a=rwkv_k_a, rwkv_r_k=rwkv_r_k, rwkv_ln_g=rwkv_ln_g, rwkv_ln_b=rwkv_ln_b,
             attn_q_norm=attn_q_norm, attn_k_norm=attn_k_norm, lru_conv_w=lru_conv_w,
             lru_conv_b=lru_conv_b, lru_gate_w=lru_gate_w, lru_gate_b=lru_gate_b,
             lru_lambda=lru_lambda, swa_sink=swa_sink, final_g=final_g)
    layers = [_prep_layer(l, P) for l in range(DEPTH)]
    return _trunk(x_prompt, P, layers), _trunk(x_sample, P, layers)
```

```python
import functools
import math

import jax
import jax.numpy as jnp
import numpy as np
from jax import lax
from jax.experimental import pallas as pl
from jax.experimental.pallas import tpu as pltpu

D_MODEL = 1024
DEPTH = 4
GRID_W = 64
HEAD_DIM = 64
GROUP_W = 256
N_HEADS_G = 4
N_KV = 2
REP = 2
KV_W = 128
RWKV_DECAY_RANK = 64
RWKV_SHIFT_W = 896
LRU_C = 8.0
LRU_BLOCKS = 4
LRU_BLK = 64
CONV_W = 4
CONV_LEFT = 2
Q_BLOCK = 128
WINDOW = 128
ROPE_THETA = 10000.0
NORM_EPS = 1e-6
GN_EPS = 64e-5
NEG = -1e30
A_W = 1152
B_W = 768
C_W = 512
D_W = 768

CHUNK = 64
RWKV_GROUP = 4
ROW_TILE = 512
HALO = 8
V7X_VMEM_LIMIT = 56 * 1024 * 1024
F32 = jnp.float32
BF16 = jnp.bfloat16
HI = lax.Precision.HIGHEST
EXP_M05 = math.exp(-0.5)


def _dot(a, b, prec=None):
    return jnp.dot(a, b, preferred_element_type=F32, precision=prec)


def _dot_nt(a, b, prec=None):
    return lax.dot_general(a, b, (((1,), (1,)), ((), ())), preferred_element_type=F32, precision=prec)


def _dot_tn(a, b, prec=None):
    return lax.dot_general(a, b, (((0,), (0,)), ((), ())), preferred_element_type=F32, precision=prec)


def _sigmoid(x):
    return 1.0 / (1.0 + jnp.exp(-x))


def _silu(x):
    return x * _sigmoid(x)


def _rms(x, g):
    return x * lax.rsqrt(jnp.mean(x * x, axis=-1, keepdims=True) + NORM_EPS) * g


def _params(n_axes=1):
    return pltpu.CompilerParams(dimension_semantics=("arbitrary",) * n_axes,
                                vmem_limit_bytes=V7X_VMEM_LIMIT)


def _full(a):
    nd = a.ndim
    return pl.BlockSpec(a.shape, lambda *_: (0,) * nd)


def _norm_kernel(x_ref, g_ref, h_ref):
    h_ref[...] = _rms(x_ref[...], g_ref[...]).astype(BF16)


def _norm(x2, g):
    m = x2.shape[0]
    tm = min(ROW_TILE, m)
    return pl.pallas_call(
        _norm_kernel,
        grid=(m // tm,),
        in_specs=[pl.BlockSpec((tm, D_MODEL), lambda i: (i, 0)), _full(g)],
        out_specs=pl.BlockSpec((tm, D_MODEL), lambda i: (i, 0)),
        out_shape=jax.ShapeDtypeStruct((m, D_MODEL), BF16),
        compiler_params=_params(),
        name="norm",
    )(x2, g)


def _out_kernel(oa_ref, ob_ref, oc_ref, od_ref, w_ref, x_ref, g_ref, *out_refs, final):
    acc = _dot(oa_ref[...], w_ref[0:GROUP_W, :])
    acc += _dot(ob_ref[...], w_ref[GROUP_W:2 * GROUP_W, :])
    acc += _dot(oc_ref[...], w_ref[2 * GROUP_W:3 * GROUP_W, :])
    acc += _dot(od_ref[...], w_ref[3 * GROUP_W:4 * GROUP_W, :])
    xn = x_ref[...] + acc
    if final:
        out_refs[0][...] = _rms(xn, g_ref[...])
    else:
        out_refs[0][...] = xn
        out_refs[1][...] = _rms(xn, g_ref[...]).astype(BF16)


def _out_proj(oa, ob, oc, od, w, x2, g, final):
    m = x2.shape[0]
    tm = min(ROW_TILE, m)
    ospec = pl.BlockSpec((tm, GROUP_W), lambda i: (i, 0))
    xspec = pl.BlockSpec((tm, D_MODEL), lambda i: (i, 0))
    if final:
        out_shape = (jax.ShapeDtypeStruct((m, D_MODEL), F32),)
        out_specs = (xspec,)
    else:
        out_shape = (jax.ShapeDtypeStruct((m, D_MODEL), F32), jax.ShapeDtypeStruct((m, D_MODEL), BF16))
        out_specs = (xspec, xspec)
    return pl.pallas_call(
        functools.partial(_out_kernel, final=final),
        grid=(m // tm,),
        in_specs=[ospec, ospec, ospec, ospec, _full(w), xspec, _full(g)],
        out_specs=out_specs,
        out_shape=out_shape,
        compiler_params=_params(),
        name="out_proj",
    )(oa, ob, oc, od, w, x2, g)


def _head_blockdiag():
    i = np.arange(GROUP_W)
    return (i[:, None] // HEAD_DIM == i[None, :] // HEAD_DIM).astype(np.float32)


def _rope_tables(t_):
    rows = t_ // GRID_W
    row = np.repeat(np.arange(rows), GRID_W).astype(np.float32)
    col = np.tile(np.arange(GRID_W), rows).astype(np.float32)
    half = HEAD_DIM // 2
    inv = (ROPE_THETA ** (-np.arange(0, half, 2, dtype=np.float32) / half)).astype(np.float32)
    ang_r = row[:, None] * inv
    ang_c = col[:, None] * inv
    cos = np.concatenate([np.cos(ang_r), np.cos(ang_r), np.cos(ang_c), np.cos(ang_c)], -1)
    sin = np.concatenate([-np.sin(ang_r), np.sin(ang_r), -np.sin(ang_c), np.sin(ang_c)], -1)
    return (np.tile(cos, (1, 2)).astype(np.float32), np.tile(sin, (1, 2)).astype(np.float32))


def _kv_select():
    sel = np.zeros((N_HEADS_G, KV_W, GROUP_W), np.float32)
    for h in range(N_HEADS_G):
        g = h // REP
        for d in range(HEAD_DIM):
            sel[h, g * HEAD_DIM + d, h * HEAD_DIM + d] = 1.0
    return sel


def _swap16(x):
    n = x.shape[-1]
    up = pltpu.roll(x, n - 16, axis=1)
    dn = pltpu.roll(x, 16, axis=1)
    lane = lax.broadcasted_iota(jnp.int32, x.shape, 1)
    return jnp.where((lane % 32) < 16, up, dn)


def _gattn_kernel(h_ref, w_ref, qw_ref, kw_ref, cos_ref, sin_ref, seg_ref, sel_ref, o_ref,
                  q_s, kw_s, vw_s, *, t_, tq, tr):
    seg = seg_ref[...]

    def prologue(i, carry):
        r0 = pl.multiple_of(i * tr, tr)
        rows = pl.ds(r0, tr)
        h = h_ref[0, rows, :]
        cos = cos_ref[rows, :]
        sin = sin_ref[rows, :]

        def normrope(z, wgt):
            ms = _dot(z * z, seg, HI) * (1.0 / HEAD_DIM)
            z = z * lax.rsqrt(ms + NORM_EPS) * wgt
            return z * cos + _swap16(z) * sin

        for half in range(2):
            zq = _dot(h, w_ref[:, half * 128:(half + 1) * 128])
            qr = normrope(zq, qw_ref[...]) * (HEAD_DIM ** -0.5)
            q_s[rows, half * 128:(half + 1) * 128] = qr.astype(BF16)
        zk = _dot(h, w_ref[:, GROUP_W:GROUP_W + KV_W])
        kr = normrope(zk, kw_ref[...]).astype(BF16)
        zv = _dot(h, w_ref[:, GROUP_W + KV_W:GROUP_W + 2 * KV_W]).astype(BF16)
        for hd in range(N_HEADS_G):
            kw_s[hd, rows, :] = _dot(kr, sel_ref[hd]).astype(BF16)
            vw_s[hd, rows, :] = _dot(zv, sel_ref[hd]).astype(BF16)
        return carry

    lax.fori_loop(0, t_ // tr, prologue, 0)

    def qblock(i, carry):
        r0 = pl.multiple_of(i * tq, tq)
        q = q_s[pl.ds(r0, tq), :]
        acc = jnp.zeros((tq, GROUP_W), F32)
        for hd in range(N_HEADS_G):
            s = _dot_nt(q, kw_s[hd])
            m = jnp.max(s, axis=-1, keepdims=True)
            p = jnp.exp(s - m)
            l = jnp.sum(p, axis=-1, keepdims=True)
            acc += _dot(p.astype(BF16), vw_s[hd]) * (1.0 / l)
        gate = _dot(h_ref[0, pl.ds(r0, tq), :], w_ref[:, GROUP_W + 2 * KV_W:])
        o_ref[0, pl.ds(r0, tq), :] = (acc * _silu(gate)).astype(BF16)
        return carry

    lax.fori_loop(0, t_ // tq, qblock, 0)


def _gattn(h3, w, qw, kw, tabs):
    b_, t_, _ = h3.shape
    tq = min(256, t_)
    cos, sin, seg, sel = tabs["cos"], tabs["sin"], tabs["seg128"], tabs["sel"]
    return pl.pallas_call(
        functools.partial(_gattn_kernel, t_=t_, tq=tq, tr=min(ROW_TILE, t_)),
        grid=(b_,),
        in_specs=[pl.BlockSpec((1, t_, D_MODEL), lambda b: (b, 0, 0)), _full(w), _full(qw), _full(kw),
                  _full(cos), _full(sin), _full(seg), _full(sel)],
        out_specs=pl.BlockSpec((1, t_, GROUP_W), lambda b: (b, 0, 0)),
        out_shape=jax.ShapeDtypeStruct((b_, t_, GROUP_W), BF16),
        scratch_shapes=[pltpu.VMEM((t_, GROUP_W), BF16),
                        pltpu.VMEM((N_HEADS_G, t_, GROUP_W), BF16),
                        pltpu.VMEM((N_HEADS_G, t_, GROUP_W), BF16)],
        compiler_params=_params(),
        name="gattn",
    )(h3, w, qw, kw, cos, sin, seg, sel)


def _wattn_kernel(sink_ref, h_ref, w_ref, bias_ref, sel_ref, o_ref, q_s, kp_s, vp_s, *, t_, tr):
    nb = t_ // Q_BLOCK
    zpad = jnp.zeros((WINDOW, GROUP_W), BF16)
    for hd in range(N_HEADS_G):
        kp_s[hd, 0:WINDOW, :] = zpad
        kp_s[hd, WINDOW + t_:, :] = zpad
        vp_s[hd, 0:WINDOW, :] = zpad
        vp_s[hd, WINDOW + t_:, :] = zpad

    def prologue(i, carry):
        r0 = pl.multiple_of(i * tr, tr)
        rows = pl.ds(r0, tr)
        prow = pl.ds(pl.multiple_of(WINDOW + r0, WINDOW), tr)
        h = h_ref[0, rows, :]
        q_s[rows, :] = (_dot(h, w_ref[:, 0:GROUP_W]) * (HEAD_DIM ** -0.5)).astype(BF16)
        zk = _dot(h, w_ref[:, GROUP_W:GROUP_W + KV_W]).astype(BF16)
        zv = _dot(h, w_ref[:, GROUP_W + KV_W:GROUP_W + 2 * KV_W]).astype(BF16)
        for hd in range(N_HEADS_G):
            kp_s[hd, prow, :] = _dot(zk, sel_ref[hd]).astype(BF16)
            vp_s[hd, prow, :] = _dot(zv, sel_ref[hd]).astype(BF16)
        return carry

    lax.fori_loop(0, t_ // tr, prologue, 0)

    kw = 3 * Q_BLOCK
    col = lax.broadcasted_iota(jnp.int32, (Q_BLOCK, kw), 1)
    row = lax.broadcasted_iota(jnp.int32, (Q_BLOCK, kw), 0)
    band = jnp.abs(col - WINDOW - row) <= WINDOW

    def qblock(i, carry):
        r0 = pl.multiple_of(i * Q_BLOCK, Q_BLOCK)
        q = q_s[pl.ds(r0, Q_BLOCK), :]
        kpos = col + (i - 1) * Q_BLOCK
        valid = band & (kpos >= 0) & (kpos < t_)
        acc = jnp.zeros((Q_BLOCK, GROUP_W), F32)
        for hd in range(N_HEADS_G):
            s = _dot_nt(q, kp_s[hd, pl.ds(r0, kw), :])
            s = jnp.where(valid, s + bias_ref[hd], NEG)
            sk = sink_ref[hd]
            m = jnp.maximum(jnp.max(s, axis=-1, keepdims=True), sk)
            p = jnp.exp(s - m)
            den = jnp.sum(p, axis=-1, keepdims=True) + jnp.exp(sk - m)
            acc += _dot(p.astype(BF16), vp_s[hd, pl.ds(r0, kw), :]) * (1.0 / den)
        gate = _dot(h_ref[0, pl.ds(r0, Q_BLOCK), :], w_ref[:, GROUP_W + 2 * KV_W:])
        o_ref[0, pl.ds(r0, Q_BLOCK), :] = (acc * _silu(gate)).astype(BF16)
        return carry

    lax.fori_loop(0, nb, qblock, 0)


def _wattn(h3, w, sink, tabs):
    b_, t_, _ = h3.shape
    bias, sel = tabs["wbias"], tabs["sel"]
    return pl.pallas_call(
        functools.partial(_wattn_kernel, t_=t_, tr=min(ROW_TILE, t_)),
        grid=(b_,),
        in_specs=[pl.BlockSpec(memory_space=pltpu.SMEM),
                  pl.BlockSpec((1, t_, D_MODEL), lambda b: (b, 0, 0)), _full(w), _full(bias), _full(sel)],
        out_specs=pl.BlockSpec((1, t_, GROUP_W), lambda b: (b, 0, 0)),
        out_shape=jax.ShapeDtypeStruct((b_, t_, GROUP_W), BF16),
        scratch_shapes=[pltpu.VMEM((t_, GROUP_W), BF16),
                        pltpu.VMEM((N_HEADS_G, t_ + 2 * WINDOW, GROUP_W), BF16),
                        pltpu.VMEM((N_HEADS_G, t_ + 2 * WINDOW, GROUP_W), BF16)],
        compiler_params=_params(),
        name="wattn",
    )(sink, h3, w, bias, sel)


def _tile_scan(a, b, rev):
    sub = lax.broadcasted_iota(jnp.int32, a.shape, 0)
    for s in (1, 2, 4):
        if rev:
            a_sh = pltpu.roll(a, 8 - s, axis=0)
            b_sh = pltpu.roll(b, 8 - s, axis=0)
            ok = sub < 8 - s
        else:
            a_sh = pltpu.roll(a, s, axis=0)
            b_sh = pltpu.roll(b, s, axis=0)
            ok = sub >= s
        b = jnp.where(ok, a * b_sh + b, b)
        a = jnp.where(ok, a * a_sh, a)
    return a, b


def _lru_kernel(h_ref, w_ref, cw_ref, cb_ref, gw_ref, gb_ref, lam_ref, o_ref, xp_s, a_s, b_s, hs_s,
                *, t_, tr):
    zhalo = jnp.zeros((HALO, GROUP_W), F32)
    xp_s[0:HALO, :] = zhalo
    xp_s[HALO + t_:, :] = zhalo

    def proj(i, carry):
        r0 = pl.multiple_of(i * tr, tr)
        xp_s[pl.ds(pl.multiple_of(HALO + r0, HALO), tr), :] = _dot(h_ref[0, pl.ds(r0, tr), :], w_ref[:, 0:GROUP_W])
        return carry

    lax.fori_loop(0, t_ // tr, proj, 0)

    lam = lam_ref[...]
    sp = jnp.maximum(-lam, 0.0) + jnp.log(1.0 + jnp.exp(-jnp.abs(lam)))

    def gates(i, carry):
        r0 = pl.multiple_of(i * tr, tr)
        rows = pl.ds(r0, tr)
        win = xp_s[pl.ds(r0, tr + 2 * HALO), :]
        xc = cb_ref[...]
        for j in range(CONV_W):
            st = HALO + j - CONV_LEFT
            xc = xc + cw_ref[j:j + 1, :] * win[st:st + tr, :]
        g = _dot(xc.astype(BF16), gw_ref[...]) + gb_ref[...]
        for d in range(2):
            r = _sigmoid(g[:, (2 * d) * GROUP_W:(2 * d + 1) * GROUP_W])
            ig = _sigmoid(g[:, (2 * d + 1) * GROUP_W:(2 * d + 2) * GROUP_W])
            a = jnp.exp(-LRU_C * r * sp[d:d + 1, :])
            a_s[d, rows, :] = a
            b_s[d, rows, :] = jnp.sqrt(1.0 - a * a) * (ig * xc)
        return carry

    lax.fori_loop(0, t_ // tr, gates, 0)

    nt = t_ // 8

    def tile(i, carry):
        cf, cr = carry
        rf = pl.multiple_of(i * 8, 8)
        rr = pl.multiple_of((nt - 1 - i) * 8, 8)
        af, bf = _tile_scan(a_s[0, pl.ds(rf, 8), :], b_s[0, pl.ds(rf, 8), :], False)
        ar, br = _tile_scan(a_s[1, pl.ds(rr, 8), :], b_s[1, pl.ds(rr, 8), :], True)
        hf = bf + af * cf
        hr = br + ar * cr
        hs_s[0, pl.ds(rf, 8), :] = hf
        hs_s[1, pl.ds(rr, 8), :] = hr
        return hf[7:8, :], hr[0:1, :]

    z = jnp.zeros((1, GROUP_W), F32)
    lax.fori_loop(0, nt, tile, (z, z))

    def epilogue(i, carry):
        r0 = pl.multiple_of(i * tr, tr)
        rows = pl.ds(r0, tr)
        gate = _dot(h_ref[0, rows, :], w_ref[:, GROUP_W:])
        o_ref[0, rows, :] = ((hs_s[0, rows, :] + hs_s[1, rows, :]) * _silu(gate)).astype(BF16)
        return carry

    lax.fori_loop(0, t_ // tr, epilogue, 0)


def _lru(h3, w, cw, cb, gw, gb, lam):
    b_, t_, _ = h3.shape
    return pl.pallas_call(
        functools.partial(_lru_kernel, t_=t_, tr=min(256, t_)),
        grid=(b_,),
        in_specs=[pl.BlockSpec((1, t_, D_MODEL), lambda b: (b, 0, 0)), _full(w), _full(cw), _full(cb),
                  _full(gw), _full(gb), _full(lam)],
        out_specs=pl.BlockSpec((1, t_, GROUP_W), lambda b: (b, 0, 0)),
        out_shape=jax.ShapeDtypeStruct((b_, t_, GROUP_W), BF16),
        scratch_shapes=[pltpu.VMEM((t_ + 2 * HALO, GROUP_W), F32),
                        pltpu.VMEM((2, t_, GROUP_W), F32), pltpu.VMEM((2, t_, GROUP_W), F32),
                        pltpu.VMEM((2, t_, GROUP_W), F32)],
        compiler_params=_params(),
        name="lru",
    )(h3, w, cw, cb, gw, gb, lam)


def _rwkv_tables():
    bd = _head_blockdiag()
    t = np.arange(CHUNK)
    tri = np.stack([(t[:, None] >= t[None, :]), (t[:, None] <= t[None, :])]).astype(np.float32)
    strict = np.stack([(t[:, None] > t[None, :]), (t[:, None] < t[None, :])]).astype(np.float32)
    incl = tri.copy()
    eye = np.eye(CHUNK, dtype=np.float32)
    wide = lambda m: np.tile(m, (1,) * (m.ndim - 1) + (N_HEADS_G,))
    return dict(bd=bd, bdb=bd, tri2=np.concatenate([tri, tri], axis=-1), strict=wide(strict), incl=wide(incl),
                eye=wide(eye))


def _rwkv_kernel(h_ref, w_ref, sh_ref, w0_ref, wup_ref, a0_ref, aup_ref, kk_ref, ka_ref, rk_ref,
                 lng_ref, lnb_ref, bd_ref, bdb_ref, tri_ref, strict_ref, incl_ref, eye_ref, o_ref,
                 zs_s, r_s, v_s, lg_s, pt_s, qd_s, kd_s, y_s, st_s, *, t_, tr):
    bd = bd_ref[...]
    bdb = bdb_ref[...]
    g = GROUP_W
    sw = RWKV_SHIFT_W
    cc = CHUNK
    bf = lambda x: x.astype(BF16)

    zhalo = jnp.zeros((HALO, sw), F32)
    zs_s[0:HALO, :] = zhalo
    zs_s[HALO + t_:, :] = zhalo

    def proj(i, carry):
        r0 = pl.multiple_of(i * tr, tr)
        zs_s[pl.ds(pl.multiple_of(HALO + r0, HALO), tr), :] = _dot(h_ref[0, pl.ds(r0, tr), :], w_ref[:, 0:sw])
        return carry

    lax.fori_loop(0, t_ // tr, proj, 0)

    def prep(i, carry):
        r0 = pl.multiple_of(i * tr, tr)
        rows = pl.ds(r0, tr)

        def mixed(c0, c1):
            win = zs_s[pl.ds(r0, tr + 2 * HALO), c0:c1]
            x = win[HALO:HALO + tr, :]
            prev = win[HALO - 1:HALO - 1 + tr, :]
            nxt = win[HALO + 1:HALO + 1 + tr, :]
            return x + sh_ref[0:1, c0:c1] * (prev - x) + sh_ref[1:2, c0:c1] * (nxt - x)

        r_s[rows, :] = mixed(0, g)
        v_s[rows, :] = mixed(2 * g, 3 * g)
        k = mixed(g, 2 * g)
        lo = mixed(3 * g, sw)
        kk = k * kk_ref[...]
        ssq = _dot(bf(kk * kk), bdb)
        kk = kk / jnp.maximum(jnp.sqrt(ssq), 1e-12)
        dw = _dot(bf(jnp.tanh(lo)), wup_ref[...])
        da = _dot(bf(lo), aup_ref[...])
        for d in range(2):
            lw = -EXP_M05 * _sigmoid(w0_ref[d:d + 1, :] + dw[:, d * g:(d + 1) * g])
            a = _sigmoid(a0_ref[d:d + 1, :] + da[:, d * g:(d + 1) * g])
            kd_s[d, rows, :] = k * (1.0 + (a - 1.0) * ka_ref[...])
            qd_s[d, rows, :] = kk * a
            lw_hi = bf(lw)
            lw_lo = bf(lw - lw_hi.astype(F32))
            lgs = [_dot(tri_ref[d], jnp.concatenate([lw_hi[c0:c0 + cc, :], lw_lo[c0:c0 + cc, :]], axis=0))
                   for c0 in range(0, tr, cc)]
            lg = jnp.concatenate(lgs, axis=0) if len(lgs) > 1 else lgs[0]
            lg_s[d, rows, :] = lg
            pt_s[d, rows, :] = kk * jnp.exp(lg - lw)
        return carry

    lax.fori_loop(0, t_ // tr, prep, 0)
    st_s[...] = jnp.zeros((2, g, g), F32)

    nc = t_ // cc
    ug = min(RWKV_GROUP, nc)
    rep = lambda x: jnp.concatenate([bf(x)] * N_HEADS_G, axis=0) * bdb
    stack = lambda a, b: bf(jnp.concatenate([a, b], axis=0))
    eye = eye_ref[...]

    def group(i, carry):
        ch = []
        for j in range(ug):
            for d in range(2):
                c = i * ug + j
                c = c if d == 0 else nc - 1 - c
                rows = pl.ds(pl.multiple_of(c * cc, cc), cc)
                lg = lg_s[d, rows, :]
                tot = lg[cc - 1:cc, :] if d == 0 else lg[0:1, :]
                g_inv = jnp.exp(-lg)
                g_end = jnp.exp(tot - lg)
                qdc = qd_s[d, rows, :]
                kdc = kd_s[d, rows, :]
                ch.append(dict(d=d, rows=rows, gtot=jnp.exp(tot), pt=pt_s[d, rows, :],
                               rt=r_s[rows, :] * jnp.exp(lg), qt=qdc * g_inv, kt=kdc * g_inv,
                               qe=qdc * g_end, ke=kdc * g_end, vc=v_s[rows, :]))
        for s in ch:
            pr = stack(s["pt"], s["rt"])
            a_q = _dot_nt(pr, rep(s["qt"]))
            a_k = _dot_nt(pr, rep(s["kt"]))
            strict = strict_ref[s["d"]]
            incl = incl_ref[s["d"]]
            s["a_rq"] = bf(a_q[cc:, :] * incl)
            s["a_k"] = stack(a_k[:cc, :] * strict, a_k[cc:, :] * incl)
            s["x"] = -(a_q[:cc, :] * strict)
            s["tm"] = eye + s["x"]
        for _ in range(5):
            for s in ch:
                s["x"] = _dot(bf(s["x"]), rep(s["x"]))
            for s in ch:
                s["tm"] = s["tm"] + _dot(bf(s["tm"]), rep(s["x"]))
        for s in ch:
            s["tmb"] = bf(s["tm"])
            s["wm"] = _dot(s["tmb"], rep(s["pt"]))
            s["av"] = _dot(s["a_k"], rep(s["vc"]))
        for s in ch:
            s["u0"] = _dot(s["tmb"], rep(s["av"][:cc, :]))
            s["rm"] = bf(s["rt"] - _dot(s["a_rq"], rep(s["wm"])))
            s["mm"] = bf(_dot_tn(s["wm"], s["qe"]) * bd)
        for s in ch:
            s["y0"] = s["av"][cc:, :] - _dot(s["a_rq"], rep(s["u0"]))
            s["nn"] = _dot_tn(jnp.concatenate([s["vc"], -s["u0"]], axis=0),
                              jnp.concatenate([s["ke"], s["qe"]], axis=0)) * bd
        sts = [st_s[0], st_s[1]]
        for j in range(ug):
            for d in range(2):
                s = ch[2 * j + d]
                sb = bf(sts[d])
                y_s[d, s["rows"], :] = _dot_nt(s["rm"], sb) + s["y0"]
                sts[d] = sts[d] * s["gtot"] - _dot(sb, s["mm"]) + s["nn"]
        st_s[0] = sts[0]
        st_s[1] = sts[1]
        return carry

    lax.fori_loop(0, nc // ug, group, 0)

    def epilogue(i, carry):
        r0 = pl.multiple_of(i * tr, tr)
        rows = pl.ds(r0, tr)
        y = y_s[0, rows, :] + y_s[1, rows, :]
        mu = _dot(bf(y), bdb) * (1.0 / HEAD_DIM)
        yc = y - mu
        var = _dot(bf(yc * yc), bdb) * (1.0 / HEAD_DIM)
        ksum = kd_s[0, rows, :] + kd_s[1, rows, :]
        bonus = _dot(bf(r_s[rows, :] * ksum * rk_ref[...]), bdb) * v_s[rows, :]
        yn = yc * lax.rsqrt(var + GN_EPS) * lng_ref[...] + lnb_ref[...] + bonus
        gate = _dot(h_ref[0, rows, :], w_ref[:, sw:])
        o_ref[0, rows, :] = (yn * _silu(gate)).astype(BF16)
        return carry

    lax.fori_loop(0, t_ // tr, epilogue, 0)


def _rwkv(h3, w, p, tabs):
    b_, t_, _ = h3.shape
    args = [h3, w, p["shift"], p["w0"], p["wup"], p["a0"], p["aup"], p["kk"], p["ka"], p["rk"], p["lng"],
            p["lnb"], tabs["bd"], tabs["bdb"], tabs["tri2"], tabs["strict"], tabs["incl"], tabs["eye"]]
    big = lambda: pltpu.VMEM((t_, GROUP_W), F32)
    big2 = lambda: pltpu.VMEM((2, t_, GROUP_W), F32)
    return pl.pallas_call(
        functools.partial(_rwkv_kernel, t_=t_, tr=min(128, t_)),
        grid=(b_,),
        in_specs=[pl.BlockSpec((1, t_, D_MODEL), lambda b: (b, 0, 0))] + [_full(a) for a in args[1:]],
        out_specs=pl.BlockSpec((1, t_, GROUP_W), lambda b: (b, 0, 0)),
        out_shape=jax.ShapeDtypeStruct((b_, t_, GROUP_W), BF16),
        scratch_shapes=[pltpu.VMEM((t_ + 2 * HALO, RWKV_SHIFT_W), F32),
                        big(), big(), big2(), big2(), big2(), big2(), big2(),
                        pltpu.VMEM((2, GROUP_W, GROUP_W), F32)],
        compiler_params=_params(),
        name="rwkv",
    )(*args)


def _prep_layer(l, P):
    w_in = P["w_in"][l].astype(BF16)
    c0, c1, c2 = A_W, A_W + B_W, A_W + B_W + C_W
    pad = jnp.zeros((2, RWKV_DECAY_RANK, GROUP_W), F32)
    wup = jnp.concatenate([P["rwkv_w_up"][l], pad], axis=1)
    aup = jnp.concatenate([pad, P["rwkv_a_up"][l]], axis=1)
    cat2 = lambda m: jnp.concatenate([m[0], m[1]], axis=1).astype(BF16)
    gw = P["lru_gate_w"][l]
    blocks = []
    for d in range(2):
        for kk in range(2):
            blocks.append(jax.scipy.linalg.block_diag(*[gw[d, kk, n] for n in range(LRU_BLOCKS)]))
    row = lambda a: a.reshape(1, -1)
    return dict(
        wA=w_in[:, :c0], wB=w_in[:, c0:c1], wC=w_in[:, c1:c2], wD=w_in[:, c2:],
        w_out=P["w_out"][l].astype(BF16),
        rwkv=dict(shift=P["rwkv_shift"][l], w0=P["rwkv_w0"][l], wup=cat2(wup), a0=P["rwkv_a0"][l],
                  aup=cat2(aup), kk=row(P["rwkv_k_k"][l]), ka=row(P["rwkv_k_a"][l]),
                  rk=row(P["rwkv_r_k"][l]), lng=row(P["rwkv_ln_g"][l]), lnb=row(P["rwkv_ln_b"][l])),
        qw=jnp.tile(P["attn_q_norm"][l], 2).reshape(1, -1), kw=jnp.tile(P["attn_k_norm"][l], 2).reshape(1, -1),
        cw=P["lru_conv_w"][l], cb=row(P["lru_conv_b"][l]),
        gw=jnp.concatenate(blocks, axis=1).astype(BF16), gb=P["lru_gate_b"][l].reshape(1, -1),
        lam=P["lru_lambda"][l], sink=P["swa_sink"][l],
    )


def _tables(t_):
    cos, sin = _rope_tables(t_)
    qi = np.arange(Q_BLOCK)[:, None]
    ki = np.arange(3 * Q_BLOCK)[None, :]
    dist = np.abs(ki - WINDOW - qi).astype(np.float32)
    slopes = np.exp2(-8.0 * np.arange(1, N_HEADS_G + 1, dtype=np.float32) / N_HEADS_G).astype(np.float32)
    wbias = (-slopes[:, None, None] * dist[None]).astype(np.float32)
    tabs = dict(cos=cos, sin=sin, seg128=_head_blockdiag()[:128, :128], sel=_kv_select(), wbias=wbias)
    tabs.update(_rwkv_tables())
    out = {k: jnp.asarray(v) for k, v in tabs.items()}
    for k in ("sel", "bdb", "tri2"):
        out[k] = out[k].astype(BF16)
    return out


def _trunk(x, P, layers):
    b_, t_, _ = x.shape
    tabs = _tables(t_)
    x2 = x.reshape(b_ * t_, D_MODEL)
    h2 = _norm(x2, P["norm_g"][0].reshape(1, -1))
    for l in range(DEPTH):
        lp = layers[l]
        h3 = h2.reshape(b_, t_, D_MODEL)
        oa = _rwkv(h3, lp["wA"], lp["rwkv"], tabs)
        ob = _gattn(h3, lp["wB"], lp["qw"], lp["kw"], tabs)
        oc = _lru(h3, lp["wC"], lp["cw"], lp["cb"], lp["gw"], lp["gb"], lp["lam"])
        od = _wattn(h3, lp["wD"], lp["sink"], tabs)
        flat = lambda o: o.reshape(b_ * t_, GROUP_W)
        final = l == DEPTH - 1
        g_next = (P["final_g"] if final else P["norm_g"][l + 1]).reshape(1, -1)
        outs = _out_proj(flat(oa), flat(ob), flat(oc), flat(od), lp["w_out"], x2, g_next, final)
        if final:
            x2 = outs[0]
        else:
            x2, h2 = outs
    return x2.reshape(b_, t_, D_MODEL)


def kernel(x_prompt, x_sample, norm_g, w_in, w_out, rwkv_shift, rwkv_w0, rwkv_w_up, rwkv_a0, rwkv_a_up,
           rwkv_k_k, rwkv_k_a, rwkv_r_k, rwkv_ln_g, rwkv_ln_b, attn_q_norm, attn_k_norm, lru_conv_w,
           lru_conv_b, lru_gate_w, lru_gate_b, lru_lambda, swa_sink, final_g):
    P = dict(norm_g=norm_g, w_in=w_in, w_out=w_out, rwkv_shift=rwkv_shift, rwkv_w0=rwkv_w0,
             rwkv_w_up=rwkv_w_up, rwkv_a0=rwkv_a0, rwkv_a_up=rwkv_a_up, rwkv_k_k=rwkv_k_k,
             rwkv_k_a=rwkv_k_a, rwkv_r_k=rwkv_r_k, rwkv_ln_g=rwkv_ln_g, rwkv_ln_b=rwkv_ln_b,
             attn_q_norm=attn_q_norm, attn_k_norm=attn_k_norm, lru_conv_w=lru_conv_w,
             lru_conv_b=lru_conv_b, lru_gate_w=lru_gate_w, lru_gate_b=lru_gate_b,
             lru_lambda=lru_lambda, swa_sink=swa_sink, final_g=final_g)
    layers = [_prep_layer(l, P) for l in range(DEPTH)]
    return _trunk(x_prompt, P, layers), _trunk(x_sample, P, layers)
```

```python
import functools
import math

import jax
import jax.numpy as jnp
import numpy as np
from jax import lax
from jax.experimental import pallas as pl
from jax.experimental.pallas import tpu as pltpu

D_MODEL = 1024
DEPTH = 4
GRID_W = 64
HEAD_DIM = 64
GROUP_W = 256
N_HEADS_G = 4
N_KV = 2
REP = 2
KV_W = 128
RWKV_DECAY_RANK = 64
RWKV_SHIFT_W = 896
LRU_C = 8.0
LRU_BLOCKS = 4
LRU_BLK = 64
CONV_W = 4
CONV_LEFT = 2
Q_BLOCK = 128
WINDOW = 128
ROPE_THETA = 10000.0
NORM_EPS = 1e-6
GN_EPS = 64e-5
NEG = -1e30
A_W = 1152
B_W = 768
C_W = 512
D_W = 768

CHUNK = 64
RWKV_GROUP = 4
WATTN_BLOCKS = 2
ROW_TILE = 512
HALO = 8
V7X_VMEM_LIMIT = 56 * 1024 * 1024
F32 = jnp.float32
BF16 = jnp.bfloat16
HI = lax.Precision.HIGHEST
EXP_M05 = math.exp(-0.5)
LOG2E = math.log2(math.e)


def _dot(a, b, prec=None):
    return jnp.dot(a, b, preferred_element_type=F32, precision=prec)


def _dot_nt(a, b, prec=None):
    return lax.dot_general(a, b, (((1,), (1,)), ((), ())), preferred_element_type=F32, precision=prec)


def _dot_tn(a, b, prec=None):
    return lax.dot_general(a, b, (((0,), (0,)), ((), ())), preferred_element_type=F32, precision=prec)


def _sigmoid(x):
    return 1.0 / (1.0 + jnp.exp(-x))


def _silu(x):
    return x * _sigmoid(x)


def _rms(x, g):
    return x * lax.rsqrt(jnp.mean(x * x, axis=-1, keepdims=True) + NORM_EPS) * g


def _params(n_axes=1):
    return pltpu.CompilerParams(dimension_semantics=("arbitrary",) * n_axes,
                                vmem_limit_bytes=V7X_VMEM_LIMIT)


def _full(a):
    nd = a.ndim
    return pl.BlockSpec(a.shape, lambda *_: (0,) * nd)


def _norm_kernel(x_ref, g_ref, h_ref):
    h_ref[...] = _rms(x_ref[...], g_ref[...]).astype(BF16)


def _norm(x2, g):
    m = x2.shape[0]
    tm = min(ROW_TILE, m)
    return pl.pallas_call(
        _norm_kernel,
        grid=(m // tm,),
        in_specs=[pl.BlockSpec((tm, D_MODEL), lambda i: (i, 0)), _full(g)],
        out_specs=pl.BlockSpec((tm, D_MODEL), lambda i: (i, 0)),
        out_shape=jax.ShapeDtypeStruct((m, D_MODEL), BF16),
        compiler_params=_params(),
        name="norm",
    )(x2, g)


def _out_kernel(oa_ref, ob_ref, oc_ref, od_ref, w_ref, x_ref, g_ref, *out_refs, final):
    acc = _dot(oa_ref[...], w_ref[0:GROUP_W, :])
    acc += _dot(ob_ref[...], w_ref[GROUP_W:2 * GROUP_W, :])
    acc += _dot(oc_ref[...], w_ref[2 * GROUP_W:3 * GROUP_W, :])
    acc += _dot(od_ref[...], w_ref[3 * GROUP_W:4 * GROUP_W, :])
    xn = x_ref[...] + acc
    if final:
        out_refs[0][...] = _rms(xn, g_ref[...])
    else:
        out_refs[0][...] = xn
        out_refs[1][...] = _rms(xn, g_ref[...]).astype(BF16)


def _out_proj(oa, ob, oc, od, w, x2, g, final):
    m = x2.shape[0]
    tm = min(ROW_TILE, m)
    ospec = pl.BlockSpec((tm, GROUP_W), lambda i: (i, 0))
    xspec = pl.BlockSpec((tm, D_MODEL), lambda i: (i, 0))
    if final:
        out_shape = (jax.ShapeDtypeStruct((m, D_MODEL), F32),)
        out_specs = (xspec,)
    else:
        out_shape = (jax.ShapeDtypeStruct((m, D_MODEL), F32), jax.ShapeDtypeStruct((m, D_MODEL), BF16))
        out_specs = (xspec, xspec)
    return pl.pallas_call(
        functools.partial(_out_kernel, final=final),
        grid=(m // tm,),
        in_specs=[ospec, ospec, ospec, ospec, _full(w), xspec, _full(g)],
        out_specs=out_specs,
        out_shape=out_shape,
        compiler_params=_params(),
        name="out_proj",
    )(oa, ob, oc, od, w, x2, g)


def _head_blockdiag():
    i = np.arange(GROUP_W)
    return (i[:, None] // HEAD_DIM == i[None, :] // HEAD_DIM).astype(np.float32)


def _rope_tables(t_):
    rows = t_ // GRID_W
    row = np.repeat(np.arange(rows), GRID_W).astype(np.float32)
    col = np.tile(np.arange(GRID_W), rows).astype(np.float32)
    half = HEAD_DIM // 2
    inv = (ROPE_THETA ** (-np.arange(0, half, 2, dtype=np.float32) / half)).astype(np.float32)
    ang_r = row[:, None] * inv
    ang_c = col[:, None] * inv
    cos = np.concatenate([np.cos(ang_r), np.cos(ang_r), np.cos(ang_c), np.cos(ang_c)], -1)
    sin = np.concatenate([-np.sin(ang_r), np.sin(ang_r), -np.sin(ang_c), np.sin(ang_c)], -1)
    return (np.tile(cos, (1, 2)).astype(np.float32), np.tile(sin, (1, 2)).astype(np.float32))


def _swap16(x):
    n = x.shape[-1]
    up = pltpu.roll(x, n - 16, axis=1)
    dn = pltpu.roll(x, 16, axis=1)
    lane = lax.broadcasted_iota(jnp.int32, x.shape, 1)
    return jnp.where((lane % 32) < 16, up, dn)


def _place_heads(half0, half1):
    lo = lax.broadcasted_iota(jnp.int32, half0.shape, 1) < HEAD_DIM
    z = jnp.zeros_like(half0)
    return [jnp.where(lo, half0, z), jnp.where(lo, pltpu.roll(half0, HEAD_DIM, axis=1), z),
            jnp.where(lo, z, pltpu.roll(half1, HEAD_DIM, axis=1)), jnp.where(lo, z, half1)]


def _gattn_kernel(h_ref, w_ref, qw_ref, kw_ref, cos_ref, sin_ref, seg_ref, o_ref,
                  qh_s, k_s, vt_s, *, t_, tq, tr):
    seg = seg_ref[...]

    def prologue(i, carry):
        r0 = pl.multiple_of(i * tr, tr)
        rows = pl.ds(r0, tr)
        h = h_ref[0, rows, :]
        cos = cos_ref[rows, :]
        sin = sin_ref[rows, :]

        def normrope(z, wgt):
            ms = _dot((z * z).astype(BF16), seg) * (1.0 / HEAD_DIM)
            z = z * lax.rsqrt(ms + NORM_EPS) * wgt
            return z * cos + _swap16(z) * sin

        z = _dot(h, w_ref[:, 0:GROUP_W + 2 * KV_W])
        halves = [normrope(z[:, c * 128:(c + 1) * 128], qw_ref[...]) * (HEAD_DIM ** -0.5 * LOG2E)
                  for c in range(2)]
        for hd, qm in enumerate(_place_heads(*halves)):
            qh_s[hd, rows, :] = qm.astype(BF16)
        k_s[rows, :] = normrope(z[:, GROUP_W:GROUP_W + KV_W], kw_ref[...]).astype(BF16)
        vt_s[i] = z[:, GROUP_W + KV_W:].T.astype(BF16)
        return carry

    lax.fori_loop(0, t_ // tr, prologue, 0)
    nvc = t_ // tr

    def qblock(i, carry):
        r0 = pl.multiple_of(i * tq, tq)
        rows = pl.ds(r0, tq)
        scores = lambda hd: _dot_nt(k_s[...], qh_s[hd, rows, :])
        outs = []
        s_next = scores(0)
        for hd in range(N_HEADS_G):
            s = s_next
            if hd + 1 < N_HEADS_G:
                s_next = scores(hd + 1)
            g0 = (hd // REP) * HEAD_DIM
            m = jnp.full((1, tq), NEG, F32)
            l = jnp.zeros((1, tq), F32)
            ot = jnp.zeros((HEAD_DIM, tq), F32)
            for c in range(nvc):
                sc = s[c * tr:(c + 1) * tr, :]
                mc = jnp.maximum(m, jnp.max(sc, axis=0, keepdims=True))
                alpha = jnp.exp2(m - mc)
                p = jnp.exp2(sc - mc)
                l = l * alpha + jnp.sum(p, axis=0, keepdims=True)
                ot = ot * alpha + _dot(vt_s[c, g0:g0 + HEAD_DIM, :], p.astype(BF16))
                m = mc
            outs.append(ot * (1.0 / l))
        o = jnp.concatenate(outs, axis=0).T
        gate = _dot(h_ref[0, rows, :], w_ref[:, GROUP_W + 2 * KV_W:])
        o_ref[0, rows, :] = (o * _silu(gate)).astype(BF16)
        return carry

    lax.fori_loop(0, t_ // tq, qblock, 0)


def _gattn(h3, w, qw, kw, tabs):
    b_, t_, _ = h3.shape
    tq = min(256, t_)
    tr = min(ROW_TILE, t_)
    cos, sin, seg = tabs["cos"], tabs["sin"], tabs["seg128"]
    return pl.pallas_call(
        functools.partial(_gattn_kernel, t_=t_, tq=tq, tr=tr),
        grid=(b_,),
        in_specs=[pl.BlockSpec((1, t_, D_MODEL), lambda b: (b, 0, 0)), _full(w), _full(qw), _full(kw),
                  _full(cos), _full(sin), _full(seg)],
        out_specs=pl.BlockSpec((1, t_, GROUP_W), lambda b: (b, 0, 0)),
        out_shape=jax.ShapeDtypeStruct((b_, t_, GROUP_W), BF16),
        scratch_shapes=[pltpu.VMEM((N_HEADS_G, t_, KV_W), BF16),
                        pltpu.VMEM((t_, KV_W), BF16),
                        pltpu.VMEM((t_ // tr, KV_W, tr), BF16)],
        compiler_params=_params(),
        name="gattn",
    )(h3, w, qw, kw, cos, sin, seg)


def _wattn_kernel(sink_ref, h_ref, w_ref, bias_ref, o_ref, qh_s, kp_s, vt_s, *, t_, tr):
    nb = t_ // Q_BLOCK
    zblk = jnp.zeros((Q_BLOCK, KV_W), BF16)
    kp_s[0:WINDOW, :] = zblk
    kp_s[WINDOW + t_:, :] = zblk
    vt_s[0] = zblk
    vt_s[nb + 1] = zblk
    bpt = tr // Q_BLOCK

    def prologue(i, carry):
        r0 = pl.multiple_of(i * tr, tr)
        rows = pl.ds(r0, tr)
        h = h_ref[0, rows, :]
        z = _dot(h, w_ref[:, 0:GROUP_W + 2 * KV_W])
        zq = z[:, 0:GROUP_W] * (HEAD_DIM ** -0.5)
        for hd, qm in enumerate(_place_heads(zq[:, 0:KV_W], zq[:, KV_W:])):
            qh_s[hd, rows, :] = qm.astype(BF16)
        kp_s[pl.ds(pl.multiple_of(WINDOW + r0, WINDOW), tr), :] = z[:, GROUP_W:GROUP_W + KV_W].astype(BF16)
        zv = z[:, GROUP_W + KV_W:]
        for jb in range(bpt):
            vt_s[1 + i * bpt + jb] = zv[jb * Q_BLOCK:(jb + 1) * Q_BLOCK, :].T.astype(BF16)
        return carry

    lax.fori_loop(0, t_ // tr, prologue, 0)

    kw = 3 * Q_BLOCK
    npb = min(WATTN_BLOCKS, nb)

    def qblocks(i, carry):
        items = []
        for j in range(npb):
            blk = i * npb + j
            r0 = pl.multiple_of(blk * Q_BLOCK, Q_BLOCK)
            var = jnp.where(blk == 0, 1, jnp.where(blk == nb - 1, 2, 0))
            kwin = kp_s[pl.ds(r0, kw), :]
            for hd in range(N_HEADS_G):
                items.append((j, hd, blk, var, _dot_nt(kwin, qh_s[hd, pl.ds(r0, Q_BLOCK), :])))
        outs = [[None] * N_HEADS_G for _ in range(npb)]
        for j, hd, blk, var, s in items:
            s = s + bias_ref[var, hd]
            sk = sink_ref[hd]
            m = jnp.maximum(jnp.max(s, axis=0, keepdims=True), sk)
            p = jnp.exp(s - m)
            den = jnp.sum(p, axis=0, keepdims=True) + jnp.exp(sk - m)
            pb = p.astype(BF16)
            g0 = (hd // REP) * HEAD_DIM
            ot = _dot(vt_s[blk, g0:g0 + HEAD_DIM, :], pb[0:Q_BLOCK, :])
            for jj in range(1, 3):
                ot += _dot(vt_s[blk + jj, g0:g0 + HEAD_DIM, :], pb[jj * Q_BLOCK:(jj + 1) * Q_BLOCK, :])
            outs[j][hd] = ot * (1.0 / den)
        for j in range(npb):
            r0 = pl.multiple_of((i * npb + j) * Q_BLOCK, Q_BLOCK)
            o = jnp.concatenate(outs[j], axis=0).T
            gate = _dot(h_ref[0, pl.ds(r0, Q_BLOCK), :], w_ref[:, GROUP_W + 2 * KV_W:])
            o_ref[0, pl.ds(r0, Q_BLOCK), :] = (o * _silu(gate)).astype(BF16)
        return carry

    lax.fori_loop(0, nb // npb, qblocks, 0)


def _wattn(h3, w, sink, tabs):
    b_, t_, _ = h3.shape
    bias = tabs["wbias"]
    nb = t_ // Q_BLOCK
    assert nb >= 2
    return pl.pallas_call(
        functools.partial(_wattn_kernel, t_=t_, tr=min(ROW_TILE, t_)),
        grid=(b_,),
        in_specs=[pl.BlockSpec(memory_space=pltpu.SMEM),
                  pl.BlockSpec((1, t_, D_MODEL), lambda b: (b, 0, 0)), _full(w), _full(bias)],
        out_specs=pl.BlockSpec((1, t_, GROUP_W), lambda b: (b, 0, 0)),
        out_shape=jax.ShapeDtypeStruct((b_, t_, GROUP_W), BF16),
        scratch_shapes=[pltpu.VMEM((N_HEADS_G, t_, KV_W), BF16),
                        pltpu.VMEM((t_ + 2 * WINDOW, KV_W), BF16),
                        pltpu.VMEM((nb + 2, KV_W, Q_BLOCK), BF16)],
        compiler_params=_params(),
        name="wattn",
    )(sink, h3, w, bias)


def _tile_scan(a, b, rev):
    sub = lax.broadcasted_iota(jnp.int32, a.shape, 0)
    for s in (1, 2, 4):
        if rev:
            a_sh = pltpu.roll(a, 8 - s, axis=0)
            b_sh = pltpu.roll(b, 8 - s, axis=0)
            ok = sub < 8 - s
        else:
            a_sh = pltpu.roll(a, s, axis=0)
            b_sh = pltpu.roll(b, s, axis=0)
            ok = sub >= s
        b = jnp.where(ok, a * b_sh + b, b)
        a = jnp.where(ok, a * a_sh, a)
    return a, b


def _lru_kernel(h_ref, w_ref, cw_ref, cb_ref, gw_ref, gb_ref, lam_ref, o_ref, xp_s, a_s, b_s, hs_s,
                *, t_, tr):
    zhalo = jnp.zeros((HALO, GROUP_W), F32)
    xp_s[0:HALO, :] = zhalo
    xp_s[HALO + t_:, :] = zhalo

    def proj(i, carry):
        r0 = pl.multiple_of(i * tr, tr)
        xp_s[pl.ds(pl.multiple_of(HALO + r0, HALO), tr), :] = _dot(h_ref[0, pl.ds(r0, tr), :], w_ref[:, 0:GROUP_W])
        return carry

    lax.fori_loop(0, t_ // tr, proj, 0)

    lam = lam_ref[...]
    sp = jnp.maximum(-lam, 0.0) + jnp.log(1.0 + jnp.exp(-jnp.abs(lam)))

    def gates(i, carry):
        r0 = pl.multiple_of(i * tr, tr)
        rows = pl.ds(r0, tr)
        win = xp_s[pl.ds(r0, tr + 2 * HALO), :]
        xc = cb_ref[...]
        for j in range(CONV_W):
            st = HALO + j - CONV_LEFT
            xc = xc + cw_ref[j:j + 1, :] * win[st:st + tr, :]
        g = _dot(xc.astype(BF16), gw_ref[...]) + gb_ref[...]
        for d in range(2):
            r = _sigmoid(g[:, (2 * d) * GROUP_W:(2 * d + 1) * GROUP_W])
            ig = _sigmoid(g[:, (2 * d + 1) * GROUP_W:(2 * d + 2) * GROUP_W])
            a = jnp.exp(-LRU_C * r * sp[d:d + 1, :])
            a_s[d, rows, :] = a
            b_s[d, rows, :] = jnp.sqrt(1.0 - a * a) * (ig * xc)
        return carry

    lax.fori_loop(0, t_ // tr, gates, 0)

    nt = t_ // 8

    def tile(i, carry):
        cf, cr = carry
        rf = pl.multiple_of(i * 8, 8)
        rr = pl.multiple_of((nt - 1 - i) * 8, 8)
        af, bf = _tile_scan(a_s[0, pl.ds(rf, 8), :], b_s[0, pl.ds(rf, 8), :], False)
        ar, br = _tile_scan(a_s[1, pl.ds(rr, 8), :], b_s[1, pl.ds(rr, 8), :], True)
        hf = bf + af * cf
        hr = br + ar * cr
        hs_s[0, pl.ds(rf, 8), :] = hf
        hs_s[1, pl.ds(rr, 8), :] = hr
        return hf[7:8, :], hr[0:1, :]

    z = jnp.zeros((1, GROUP_W), F32)
    lax.fori_loop(0, nt, tile, (z, z))

    def epilogue(i, carry):
        r0 = pl.multiple_of(i * tr, tr)
        rows = pl.ds(r0, tr)
        gate = _dot(h_ref[0, rows, :], w_ref[:, GROUP_W:])
        o_ref[0, rows, :] = ((hs_s[0, rows, :] + hs_s[1, rows, :]) * _silu(gate)).astype(BF16)
        return carry

    lax.fori_loop(0, t_ // tr, epilogue, 0)


def _lru(h3, w, cw, cb, gw, gb, lam):
    b_, t_, _ = h3.shape
    return pl.pallas_call(
        functools.partial(_lru_kernel, t_=t_, tr=min(256, t_)),
        grid=(b_,),
        in_specs=[pl.BlockSpec((1, t_, D_MODEL), lambda b: (b, 0, 0)), _full(w), _full(cw), _full(cb),
                  _full(gw), _full(gb), _full(lam)],
        out_specs=pl.BlockSpec((1, t_, GROUP_W), lambda b: (b, 0, 0)),
        out_shape=jax.ShapeDtypeStruct((b_, t_, GROUP_W), BF16),
        scratch_shapes=[pltpu.VMEM((t_ + 2 * HALO, GROUP_W), F32),
                        pltpu.VMEM((2, t_, GROUP_W), F32), pltpu.VMEM((2, t_, GROUP_W), F32),
                        pltpu.VMEM((2, t_, GROUP_W), F32)],
        compiler_params=_params(),
        name="lru",
    )(h3, w, cw, cb, gw, gb, lam)


def _rwkv_tables():
    bd = _head_blockdiag()
    t = np.arange(CHUNK)
    tri = np.stack([(t[:, None] >= t[None, :]), (t[:, None] <= t[None, :])]).astype(np.float32)
    strict = np.stack([(t[:, None] > t[None, :]), (t[:, None] < t[None, :])]).astype(np.float32)
    incl = tri.copy()
    eye = np.eye(CHUNK, dtype=np.float32)
    wide = lambda m: np.tile(m, (1,) * (m.ndim - 1) + (N_HEADS_G,))
    return dict(bd=bd, bdb=bd, tri2=np.concatenate([tri, tri], axis=-1), strict=wide(strict), incl=wide(incl),
                eye=wide(eye))


def _rwkv_kernel(h_ref, w_ref, sh_ref, w0_ref, wup_ref, a0_ref, aup_ref, kk_ref, ka_ref, rk_ref,
                 lng_ref, lnb_ref, bd_ref, bdb_ref, tri_ref, strict_ref, incl_ref, eye_ref, o_ref,
                 zs_s, r_s, v_s, lg_s, pt_s, qd_s, kd_s, y_s, st_s, *, t_, tr):
    bd = bd_ref[...]
    bdb = bdb_ref[...]
    g = GROUP_W
    sw = RWKV_SHIFT_W
    cc = CHUNK
    bf = lambda x: x.astype(BF16)

    zhalo = jnp.zeros((HALO, sw), F32)
    zs_s[0:HALO, :] = zhalo
    zs_s[HALO + t_:, :] = zhalo

    tp = min(ROW_TILE, t_)

    def proj(i, carry):
        r0 = pl.multiple_of(i * tp, tp)
        zs_s[pl.ds(pl.multiple_of(HALO + r0, HALO), tp), :] = _dot(h_ref[0, pl.ds(r0, tp), :], w_ref[:, 0:sw])
        return carry

    lax.fori_loop(0, t_ // tp, proj, 0)

    def prep(i, carry):
        r0 = pl.multiple_of(i * tr, tr)
        rows = pl.ds(r0, tr)

        def mixed(c0, c1):
            win = zs_s[pl.ds(r0, tr + 2 * HALO), c0:c1]
            x = win[HALO:HALO + tr, :]
            prev = win[HALO - 1:HALO - 1 + tr, :]
            nxt = win[HALO + 1:HALO + 1 + tr, :]
            return x + sh_ref[0:1, c0:c1] * (prev - x) + sh_ref[1:2, c0:c1] * (nxt - x)

        r_s[rows, :] = mixed(0, g)
        v_s[rows, :] = mixed(2 * g, 3 * g)
        k = mixed(g, 2 * g)
        lo = mixed(3 * g, sw)
        kk = k * kk_ref[...]
        ssq = _dot(bf(kk * kk), bdb)
        kk = kk / jnp.maximum(jnp.sqrt(ssq), 1e-12)
        dw = _dot(bf(jnp.tanh(lo)), wup_ref[...])
        da = _dot(bf(lo), aup_ref[...])
        for d in range(2):
            lw = -EXP_M05 * _sigmoid(w0_ref[d:d + 1, :] + dw[:, d * g:(d + 1) * g])
            a = _sigmoid(a0_ref[d:d + 1, :] + da[:, d * g:(d + 1) * g])
            kd_s[d, rows, :] = k * (1.0 + (a - 1.0) * ka_ref[...])
            qd_s[d, rows, :] = kk * a
            lw_hi = bf(lw)
            lw_lo = bf(lw - lw_hi.astype(F32))
            lgs = [_dot(tri_ref[d], jnp.concatenate([lw_hi[c0:c0 + cc, :], lw_lo[c0:c0 + cc, :]], axis=0))
                   for c0 in range(0, tr, cc)]
            lg = jnp.concatenate(lgs, axis=0) if len(lgs) > 1 else lgs[0]
            lg_s[d, rows, :] = lg
            pt_s[d, rows, :] = kk * jnp.exp(lg - lw)
        return carry

    lax.fori_loop(0, t_ // tr, prep, 0)
    st_s[...] = jnp.zeros((2, g, g), F32)

    nc = t_ // cc
    ug = min(RWKV_GROUP, nc)
    rep = lambda x: jnp.concatenate([bf(x)] * N_HEADS_G, axis=0) * bdb
    stack = lambda a, b: bf(jnp.concatenate([a, b], axis=0))
    eye = eye_ref[...]

    def group(i, carry):
        ch = []
        for j in range(ug):
            for d in range(2):
                c = i * ug + j
                c = c if d == 0 else nc - 1 - c
                rows = pl.ds(pl.multiple_of(c * cc, cc), cc)
                lg = lg_s[d, rows, :]
                tot = lg[cc - 1:cc, :] if d == 0 else lg[0:1, :]
                g_inv = jnp.exp(-lg)
                g_end = jnp.exp(tot - lg)
                qdc = qd_s[d, rows, :]
                kdc = kd_s[d, rows, :]
                ch.append(dict(d=d, rows=rows, gtot=jnp.exp(tot), pt=pt_s[d, rows, :],
                               rt=r_s[rows, :] * jnp.exp(lg), qt=qdc * g_inv, kt=kdc * g_inv,
                               qe=qdc * g_end, ke=kdc * g_end, vc=v_s[rows, :]))
        for s in ch:
            pr = stack(s["pt"], s["rt"])
            a_q = _dot_nt(pr, rep(s["qt"]))
            a_k = _dot_nt(pr, rep(s["kt"]))
            strict = strict_ref[s["d"]]
            incl = incl_ref[s["d"]]
            s["a_rq"] = bf(a_q[cc:, :] * incl)
            s["a_k"] = stack(a_k[:cc, :] * strict, a_k[cc:, :] * incl)
            s["x"] = -(a_q[:cc, :] * strict)
            s["tm"] = eye + s["x"]
        for _ in range(5):
            for s in ch:
                s["x"] = _dot(bf(s["x"]), rep(s["x"]))
            for s in ch:
                s["tm"] = s["tm"] + _dot(bf(s["tm"]), rep(s["x"]))
        for s in ch:
            s["tmb"] = bf(s["tm"])
            s["wm"] = _dot(s["tmb"], rep(s["pt"]))
            s["av"] = _dot(s["a_k"], rep(s["vc"]))
        for s in ch:
            s["u0"] = _dot(s["tmb"], rep(s["av"][:cc, :]))
            s["rm"] = bf(s["rt"] - _dot(s["a_rq"], rep(s["wm"])))
            s["mm"] = bf(_dot_tn(s["wm"], s["qe"]) * bd)
        for s in ch:
            s["y0"] = s["av"][cc:, :] - _dot(s["a_rq"], rep(s["u0"]))
            s["nn"] = _dot_tn(jnp.concatenate([s["vc"], -s["u0"]], axis=0),
                              jnp.concatenate([s["ke"], s["qe"]], axis=0)) * bd
        sts = [st_s[0], st_s[1]]
        for j in range(ug):
            for d in range(2):
                s = ch[2 * j + d]
                sb = bf(sts[d])
                y_s[d, s["rows"], :] = _dot_nt(s["rm"], sb) + s["y0"]
                sts[d] = sts[d] * s["gtot"] - _dot(sb, s["mm"]) + s["nn"]
        st_s[0] = sts[0]
        st_s[1] = sts[1]
        return carry

    lax.fori_loop(0, nc // ug, group, 0)

    def epilogue(i, carry):
        r0 = pl.multiple_of(i * tr, tr)
        rows = pl.ds(r0, tr)
        y = y_s[0, rows, :] + y_s[1, rows, :]
        mu = _dot(bf(y), bdb) * (1.0 / HEAD_DIM)
        yc = y - mu
        var = _dot(bf(yc * yc), bdb) * (1.0 / HEAD_DIM)
        ksum = kd_s[0, rows, :] + kd_s[1, rows, :]
        bonus = _dot(bf(r_s[rows, :] * ksum * rk_ref[...]), bdb) * v_s[rows, :]
        yn = yc * lax.rsqrt(var + GN_EPS) * lng_ref[...] + lnb_ref[...] + bonus
        gate = _dot(h_ref[0, rows, :], w_ref[:, sw:])
        o_ref[0, rows, :] = (yn * _silu(gate)).astype(BF16)
        return carry

    lax.fori_loop(0, t_ // tr, epilogue, 0)


def _rwkv(h3, w, p, tabs):
    b_, t_, _ = h3.shape
    args = [h3, w, p["shift"], p["w0"], p["wup"], p["a0"], p["aup"], p["kk"], p["ka"], p["rk"], p["lng"],
            p["lnb"], tabs["bd"], tabs["bdb"], tabs["tri2"], tabs["strict"], tabs["incl"], tabs["eye"]]
    big = lambda: pltpu.VMEM((t_, GROUP_W), F32)
    big2 = lambda: pltpu.VMEM((2, t_, GROUP_W), F32)
    return pl.pallas_call(
        functools.partial(_rwkv_kernel, t_=t_, tr=min(128, t_)),
        grid=(b_,),
        in_specs=[pl.BlockSpec((1, t_, D_MODEL), lambda b: (b, 0, 0))] + [_full(a) for a in args[1:]],
        out_specs=pl.BlockSpec((1, t_, GROUP_W), lambda b: (b, 0, 0)),
        out_shape=jax.ShapeDtypeStruct((b_, t_, GROUP_W), BF16),
        scratch_shapes=[pltpu.VMEM((t_ + 2 * HALO, RWKV_SHIFT_W), F32),
                        big(), big(), big2(), big2(), big2(), big2(), big2(),
                        pltpu.VMEM((2, GROUP_W, GROUP_W), F32)],
        compiler_params=_params(),
        name="rwkv",
    )(*args)


def _prep_layer(l, P):
    w_in = P["w_in"][l].astype(BF16)
    c0, c1, c2 = A_W, A_W + B_W, A_W + B_W + C_W
    pad = jnp.zeros((2, RWKV_DECAY_RANK, GROUP_W), F32)
    wup = jnp.concatenate([P["rwkv_w_up"][l], pad], axis=1)
    aup = jnp.concatenate([pad, P["rwkv_a_up"][l]], axis=1)
    cat2 = lambda m: jnp.concatenate([m[0], m[1]], axis=1).astype(BF16)
    gw = P["lru_gate_w"][l]
    blocks = []
    for d in range(2):
        for kk in range(2):
            blocks.append(jax.scipy.linalg.block_diag(*[gw[d, kk, n] for n in range(LRU_BLOCKS)]))
    row = lambda a: a.reshape(1, -1)
    return dict(
        wA=w_in[:, :c0], wB=w_in[:, c0:c1], wC=w_in[:, c1:c2], wD=w_in[:, c2:],
        w_out=P["w_out"][l].astype(BF16),
        rwkv=dict(shift=P["rwkv_shift"][l], w0=P["rwkv_w0"][l], wup=cat2(wup), a0=P["rwkv_a0"][l],
                  aup=cat2(aup), kk=row(P["rwkv_k_k"][l]), ka=row(P["rwkv_k_a"][l]),
                  rk=row(P["rwkv_r_k"][l]), lng=row(P["rwkv_ln_g"][l]), lnb=row(P["rwkv_ln_b"][l])),
        qw=jnp.tile(P["attn_q_norm"][l], 2).reshape(1, -1), kw=jnp.tile(P["attn_k_norm"][l], 2).reshape(1, -1),
        cw=P["lru_conv_w"][l], cb=row(P["lru_conv_b"][l]),
        gw=jnp.concatenate(blocks, axis=1).astype(BF16), gb=P["lru_gate_b"][l].reshape(1, -1),
        lam=P["lru_lambda"][l], sink=P["swa_sink"][l],
    )


def _tables(t_):
    cos, sin = _rope_tables(t_)
    qi = np.arange(Q_BLOCK)[:, None]
    ki = np.arange(3 * Q_BLOCK)[None, :]
    dist = np.abs(ki - WINDOW - qi).astype(np.float32)
    slopes = np.exp2(-8.0 * np.arange(1, N_HEADS_G + 1, dtype=np.float32) / N_HEADS_G).astype(np.float32)
    alibi = np.where(dist[None] <= WINDOW, -slopes[:, None, None] * dist[None], NEG).astype(np.float32)
    first = np.where(ki[None] < Q_BLOCK, NEG, alibi).astype(np.float32)
    last = np.where(ki[None] >= 2 * Q_BLOCK, NEG, alibi).astype(np.float32)
    wbias = np.ascontiguousarray(np.stack([alibi, first, last]).transpose(0, 1, 3, 2))
    tabs = dict(cos=cos, sin=sin, seg128=_head_blockdiag()[:128, :128], wbias=wbias)
    tabs.update(_rwkv_tables())
    out = {k: jnp.asarray(v) for k, v in tabs.items()}
    for k in ("bdb", "tri2", "seg128"):
        out[k] = out[k].astype(BF16)
    return out


def _trunk(x, P, layers):
    b_, t_, _ = x.shape
    tabs = _tables(t_)
    x2 = x.reshape(b_ * t_, D_MODEL)
    h2 = _norm(x2, P["norm_g"][0].reshape(1, -1))
    for l in range(DEPTH):
        lp = layers[l]
        h3 = h2.reshape(b_, t_, D_MODEL)
        oa = _rwkv(h3, lp["wA"], lp["rwkv"], tabs)
        ob = _gattn(h3, lp["wB"], lp["qw"], lp["kw"], tabs)
        oc = _lru(h3, lp["wC"], lp["cw"], lp["cb"], lp["gw"], lp["gb"], lp["lam"])
        od = _wattn(h3, lp["wD"], lp["sink"], tabs)
        flat = lambda o: o.reshape(b_ * t_, GROUP_W)
        final = l == DEPTH - 1
        g_next = (P["final_g"] if final else P["norm_g"][l + 1]).reshape(1, -1)
        outs = _out_proj(flat(oa), flat(ob), flat(oc), flat(od), lp["w_out"], x2, g_next, final)
        if final:
            x2 = outs[0]
        else:
            x2, h2 = outs
    return x2.reshape(b_, t_, D_MODEL)


def kernel(x_prompt, x_sample, norm_g, w_in, w_out, rwkv_shift, rwkv_w0, rwkv_w_up, rwkv_a0, rwkv_a_up,
           rwkv_k_k, rwkv_k_a, rwkv_r_k, rwkv_ln_g, rwkv_ln_b, attn_q_norm, attn_k_norm, lru_conv_w,
           lru_conv_b, lru_gate_w, lru_gate_b, lru_lambda, swa_sink, final_g):
    P = dict(norm_g=norm_g, w_in=w_in, w_out=w_out, rwkv_shift=rwkv_shift, rwkv_w0=rwkv_w0,
             rwkv_w_up=rwkv_w_up, rwkv_a0=rwkv_a0, rwkv_a_up=rwkv_a_up, rwkv_k_k=rwkv_k_k,
             rwkv_k_a=rwkv_k_a, rwkv_r_k=rwkv_r_k, rwkv_ln_g=rwkv_ln_g, rwkv_ln_b=rwkv_ln_b,
             attn_q_norm=attn_q_norm, attn_k_norm=attn_k_norm, lru_conv_w=lru_conv_w,
             lru_conv_b=lru_conv_b, lru_gate_w=lru_gate_w, lru_gate_b=lru_gate_b,
             lru_lambda=lru_lambda, swa_sink=swa_sink, final_g=final_g)
    layers = [_prep_layer(l, P) for l in range(DEPTH)]
    return _trunk(x_prompt, P, layers), _trunk(x_sample, P, layers)
```

```python
import functools
import math

import jax
import jax.numpy as jnp
import numpy as np
from jax import lax
from jax.experimental import pallas as pl
from jax.experimental.pallas import tpu as pltpu

D_MODEL = 1024
DEPTH = 4
GRID_W = 64
HEAD_DIM = 64
GROUP_W = 256
N_HEADS_G = 4
N_KV = 2
REP = 2
KV_W = 128
RWKV_DECAY_RANK = 64
RWKV_SHIFT_W = 896
LRU_C = 8.0
LRU_BLOCKS = 4
LRU_BLK = 64
CONV_W = 4
CONV_LEFT = 2
Q_BLOCK = 128
WINDOW = 128
ROPE_THETA = 10000.0
NORM_EPS = 1e-6
GN_EPS = 64e-5
NEG = -1e30
A_W = 1152
B_W = 768
C_W = 512
D_W = 768

CHUNK = 64
RWKV_GROUP = 4
LRU_UNROLL = 4
WATTN_BLOCKS = 2
ROW_TILE = 512
HALO = 8
V7X_VMEM_LIMIT = 56 * 1024 * 1024
F32 = jnp.float32
BF16 = jnp.bfloat16
HI = lax.Precision.HIGHEST
EXP_M05 = math.exp(-0.5)
LOG2E = math.log2(math.e)


def _dot(a, b, prec=None):
    return jnp.dot(a, b, preferred_element_type=F32, precision=prec)


def _dot_nt(a, b, prec=None):
    return lax.dot_general(a, b, (((1,), (1,)), ((), ())), preferred_element_type=F32, precision=prec)


def _dot_tn(a, b, prec=None):
    return lax.dot_general(a, b, (((0,), (0,)), ((), ())), preferred_element_type=F32, precision=prec)


def _sigmoid(x):
    return 1.0 / (1.0 + jnp.exp(-x))


def _silu(x):
    return x * _sigmoid(x)


def _rms(x, g):
    return x * lax.rsqrt(jnp.mean(x * x, axis=-1, keepdims=True) + NORM_EPS) * g


def _params(n_axes=1):
    return pltpu.CompilerParams(dimension_semantics=("arbitrary",) * n_axes,
                                vmem_limit_bytes=V7X_VMEM_LIMIT)


def _full(a):
    nd = a.ndim
    return pl.BlockSpec(a.shape, lambda *_: (0,) * nd)


def _norm_kernel(x_ref, g_ref, h_ref):
    h_ref[...] = _rms(x_ref[...], g_ref[...]).astype(BF16)


def _norm(x2, g):
    m = x2.shape[0]
    tm = min(ROW_TILE, m)
    return pl.pallas_call(
        _norm_kernel,
        grid=(m // tm,),
        in_specs=[pl.BlockSpec((tm, D_MODEL), lambda i: (i, 0)), _full(g)],
        out_specs=pl.BlockSpec((tm, D_MODEL), lambda i: (i, 0)),
        out_shape=jax.ShapeDtypeStruct((m, D_MODEL), BF16),
        compiler_params=_params(),
        name="norm",
    )(x2, g)


def _out_kernel(oa_ref, ob_ref, oc_ref, od_ref, w_ref, x_ref, g_ref, *out_refs, final):
    acc = _dot(oa_ref[...], w_ref[0:GROUP_W, :])
    acc += _dot(ob_ref[...], w_ref[GROUP_W:2 * GROUP_W, :])
    acc += _dot(oc_ref[...], w_ref[2 * GROUP_W:3 * GROUP_W, :])
    acc += _dot(od_ref[...], w_ref[3 * GROUP_W:4 * GROUP_W, :])
    xn = x_ref[...] + acc
    if final:
        out_refs[0][...] = _rms(xn, g_ref[...])
    else:
        out_refs[0][...] = xn
        out_refs[1][...] = _rms(xn, g_ref[...]).astype(BF16)


def _out_proj(oa, ob, oc, od, w, x2, g, final):
    m = x2.shape[0]
    tm = min(ROW_TILE, m)
    ospec = pl.BlockSpec((tm, GROUP_W), lambda i: (i, 0))
    xspec = pl.BlockSpec((tm, D_MODEL), lambda i: (i, 0))
    if final:
        out_shape = (jax.ShapeDtypeStruct((m, D_MODEL), F32),)
        out_specs = (xspec,)
    else:
        out_shape = (jax.ShapeDtypeStruct((m, D_MODEL), F32), jax.ShapeDtypeStruct((m, D_MODEL), BF16))
        out_specs = (xspec, xspec)
    return pl.pallas_call(
        functools.partial(_out_kernel, final=final),
        grid=(m // tm,),
        in_specs=[ospec, ospec, ospec, ospec, _full(w), xspec, _full(g)],
        out_specs=out_specs,
        out_shape=out_shape,
        compiler_params=_params(),
        name="out_proj",
    )(oa, ob, oc, od, w, x2, g)


def _head_blockdiag():
    i = np.arange(GROUP_W)
    return (i[:, None] // HEAD_DIM == i[None, :] // HEAD_DIM).astype(np.float32)


def _rope_tables(t_):
    rows = t_ // GRID_W
    row = np.repeat(np.arange(rows), GRID_W).astype(np.float32)
    col = np.tile(np.arange(GRID_W), rows).astype(np.float32)
    half = HEAD_DIM // 2
    inv = (ROPE_THETA ** (-np.arange(0, half, 2, dtype=np.float32) / half)).astype(np.float32)
    ang_r = row[:, None] * inv
    ang_c = col[:, None] * inv
    cos = np.concatenate([np.cos(ang_r), np.cos(ang_r), np.cos(ang_c), np.cos(ang_c)], -1)
    sin = np.concatenate([-np.sin(ang_r), np.sin(ang_r), -np.sin(ang_c), np.sin(ang_c)], -1)
    return (np.tile(cos, (1, 2)).astype(np.float32), np.tile(sin, (1, 2)).astype(np.float32))


def _swap16(x):
    n = x.shape[-1]
    up = pltpu.roll(x, n - 16, axis=1)
    dn = pltpu.roll(x, 16, axis=1)
    lane = lax.broadcasted_iota(jnp.int32, x.shape, 1)
    return jnp.where((lane % 32) < 16, up, dn)


def _place_heads(half0, half1):
    lo = lax.broadcasted_iota(jnp.int32, half0.shape, 1) < HEAD_DIM
    z = jnp.zeros_like(half0)
    return [jnp.where(lo, half0, z), jnp.where(lo, pltpu.roll(half0, HEAD_DIM, axis=1), z),
            jnp.where(lo, z, pltpu.roll(half1, HEAD_DIM, axis=1)), jnp.where(lo, z, half1)]


def _gattn_kernel(h_ref, w_ref, qw_ref, kw_ref, cos_ref, sin_ref, seg_ref, o_ref,
                  qh_s, k_s, vt_s, *, t_, tq, tr):
    seg = seg_ref[...]

    def prologue(i, carry):
        r0 = pl.multiple_of(i * tr, tr)
        rows = pl.ds(r0, tr)
        h = h_ref[0, rows, :]
        cos = cos_ref[rows, :]
        sin = sin_ref[rows, :]

        def normrope(z, wgt):
            ms = _dot((z * z).astype(BF16), seg) * (1.0 / HEAD_DIM)
            z = z * lax.rsqrt(ms + NORM_EPS) * wgt
            return z * cos + _swap16(z) * sin

        z = _dot(h, w_ref[:, 0:GROUP_W + 2 * KV_W])
        halves = [normrope(z[:, c * 128:(c + 1) * 128], qw_ref[...]) * (HEAD_DIM ** -0.5 * LOG2E)
                  for c in range(2)]
        for hd, qm in enumerate(_place_heads(*halves)):
            qh_s[hd, rows, :] = qm.astype(BF16)
        k_s[rows, :] = normrope(z[:, GROUP_W:GROUP_W + KV_W], kw_ref[...]).astype(BF16)
        vt_s[i] = z[:, GROUP_W + KV_W:].T.astype(BF16)
        return carry

    lax.fori_loop(0, t_ // tr, prologue, 0)
    nvc = t_ // tr

    def qblock(i, carry):
        r0 = pl.multiple_of(i * tq, tq)
        rows = pl.ds(r0, tq)
        scores = lambda hd: _dot_nt(k_s[...], qh_s[hd, rows, :])
        outs = []
        s_next = scores(0)
        for hd in range(N_HEADS_G):
            s = s_next
            if hd + 1 < N_HEADS_G:
                s_next = scores(hd + 1)
            g0 = (hd // REP) * HEAD_DIM
            m = jnp.full((1, tq), NEG, F32)
            l = jnp.zeros((1, tq), F32)
            ot = jnp.zeros((HEAD_DIM, tq), F32)
            for c in range(nvc):
                sc = s[c * tr:(c + 1) * tr, :]
                mc = jnp.maximum(m, jnp.max(sc, axis=0, keepdims=True))
                alpha = jnp.exp2(m - mc)
                p = jnp.exp2(sc - mc)
                l = l * alpha + jnp.sum(p, axis=0, keepdims=True)
                ot = ot * alpha + _dot(vt_s[c, g0:g0 + HEAD_DIM, :], p.astype(BF16))
                m = mc
            outs.append(ot * (1.0 / l))
        o = jnp.concatenate(outs, axis=0).T
        gate = _dot(h_ref[0, rows, :], w_ref[:, GROUP_W + 2 * KV_W:])
        o_ref[0, rows, :] = (o * _silu(gate)).astype(BF16)
        return carry

    lax.fori_loop(0, t_ // tq, qblock, 0)


def _gattn(h3, w, qw, kw, tabs):
    b_, t_, _ = h3.shape
    tq = min(256, t_)
    tr = min(ROW_TILE, t_)
    cos, sin, seg = tabs["cos"], tabs["sin"], tabs["seg128"]
    return pl.pallas_call(
        functools.partial(_gattn_kernel, t_=t_, tq=tq, tr=tr),
        grid=(b_,),
        in_specs=[pl.BlockSpec((1, t_, D_MODEL), lambda b: (b, 0, 0)), _full(w), _full(qw), _full(kw),
                  _full(cos), _full(sin), _full(seg)],
        out_specs=pl.BlockSpec((1, t_, GROUP_W), lambda b: (b, 0, 0)),
        out_shape=jax.ShapeDtypeStruct((b_, t_, GROUP_W), BF16),
        scratch_shapes=[pltpu.VMEM((N_HEADS_G, t_, KV_W), BF16),
                        pltpu.VMEM((t_, KV_W), BF16),
                        pltpu.VMEM((t_ // tr, KV_W, tr), BF16)],
        compiler_params=_params(),
        name="gattn",
    )(h3, w, qw, kw, cos, sin, seg)


def _wattn_kernel(sink_ref, h_ref, w_ref, bias_ref, o_ref, qh_s, kp_s, vt_s, *, t_, tr):
    nb = t_ // Q_BLOCK
    zblk = jnp.zeros((Q_BLOCK, KV_W), BF16)
    kp_s[0:WINDOW, :] = zblk
    kp_s[WINDOW + t_:, :] = zblk
    vt_s[0] = zblk
    vt_s[nb + 1] = zblk
    bpt = tr // Q_BLOCK

    def prologue(i, carry):
        r0 = pl.multiple_of(i * tr, tr)
        rows = pl.ds(r0, tr)
        h = h_ref[0, rows, :]
        z = _dot(h, w_ref[:, 0:GROUP_W + 2 * KV_W])
        zq = z[:, 0:GROUP_W] * (HEAD_DIM ** -0.5)
        for hd, qm in enumerate(_place_heads(zq[:, 0:KV_W], zq[:, KV_W:])):
            qh_s[hd, rows, :] = qm.astype(BF16)
        kp_s[pl.ds(pl.multiple_of(WINDOW + r0, WINDOW), tr), :] = z[:, GROUP_W:GROUP_W + KV_W].astype(BF16)
        zv = z[:, GROUP_W + KV_W:]
        for jb in range(bpt):
            vt_s[1 + i * bpt + jb] = zv[jb * Q_BLOCK:(jb + 1) * Q_BLOCK, :].T.astype(BF16)
        return carry

    lax.fori_loop(0, t_ // tr, prologue, 0)

    kw = 3 * Q_BLOCK
    npb = min(WATTN_BLOCKS, nb)

    def qblocks(i, carry):
        items = []
        for j in range(npb):
            blk = i * npb + j
            r0 = pl.multiple_of(blk * Q_BLOCK, Q_BLOCK)
            var = jnp.where(blk == 0, 1, jnp.where(blk == nb - 1, 2, 0))
            kwin = kp_s[pl.ds(r0, kw), :]
            for hd in range(N_HEADS_G):
                items.append((j, hd, blk, var, _dot_nt(kwin, qh_s[hd, pl.ds(r0, Q_BLOCK), :])))
        outs = [[None] * N_HEADS_G for _ in range(npb)]
        for j, hd, blk, var, s in items:
            s = s + bias_ref[var, hd]
            sk = sink_ref[hd]
            m = jnp.maximum(jnp.max(s, axis=0, keepdims=True), sk)
            p = jnp.exp(s - m)
            den = jnp.sum(p, axis=0, keepdims=True) + jnp.exp(sk - m)
            pb = p.astype(BF16)
            g0 = (hd // REP) * HEAD_DIM
            ot = _dot(vt_s[blk, g0:g0 + HEAD_DIM, :], pb[0:Q_BLOCK, :])
            for jj in range(1, 3):
                ot += _dot(vt_s[blk + jj, g0:g0 + HEAD_DIM, :], pb[jj * Q_BLOCK:(jj + 1) * Q_BLOCK, :])
            outs[j][hd] = ot * (1.0 / den)
        for j in range(npb):
            r0 = pl.multiple_of((i * npb + j) * Q_BLOCK, Q_BLOCK)
            o = jnp.concatenate(outs[j], axis=0).T
            gate = _dot(h_ref[0, pl.ds(r0, Q_BLOCK), :], w_ref[:, GROUP_W + 2 * KV_W:])
            o_ref[0, pl.ds(r0, Q_BLOCK), :] = (o * _silu(gate)).astype(BF16)
        return carry

    lax.fori_loop(0, nb // npb, qblocks, 0)


def _wattn(h3, w, sink, tabs):
    b_, t_, _ = h3.shape
    bias = tabs["wbias"]
    nb = t_ // Q_BLOCK
    assert nb >= 2
    return pl.pallas_call(
        functools.partial(_wattn_kernel, t_=t_, tr=min(ROW_TILE, t_)),
        grid=(b_,),
        in_specs=[pl.BlockSpec(memory_space=pltpu.SMEM),
                  pl.BlockSpec((1, t_, D_MODEL), lambda b: (b, 0, 0)), _full(w), _full(bias)],
        out_specs=pl.BlockSpec((1, t_, GROUP_W), lambda b: (b, 0, 0)),
        out_shape=jax.ShapeDtypeStruct((b_, t_, GROUP_W), BF16),
        scratch_shapes=[pltpu.VMEM((N_HEADS_G, t_, KV_W), BF16),
                        pltpu.VMEM((t_ + 2 * WINDOW, KV_W), BF16),
                        pltpu.VMEM((nb + 2, KV_W, Q_BLOCK), BF16)],
        compiler_params=_params(),
        name="wattn",
    )(sink, h3, w, bias)


def _tile_scan(a, b, rev):
    sub = lax.broadcasted_iota(jnp.int32, a.shape, 0)
    for s in (1, 2, 4):
        if rev:
            a_sh = pltpu.roll(a, 8 - s, axis=0)
            b_sh = pltpu.roll(b, 8 - s, axis=0)
            ok = sub < 8 - s
        else:
            a_sh = pltpu.roll(a, s, axis=0)
            b_sh = pltpu.roll(b, s, axis=0)
            ok = sub >= s
        b = jnp.where(ok, a * b_sh + b, b)
        a = jnp.where(ok, a * a_sh, a)
    return a, b


def _lru_kernel(h_ref, w_ref, cw_ref, cb_ref, gw_ref, gb_ref, lam_ref, o_ref, xp_s, a_s, b_s, hs_s,
                *, t_, tr):
    zhalo = jnp.zeros((HALO, GROUP_W), F32)
    xp_s[0:HALO, :] = zhalo
    xp_s[HALO + t_:, :] = zhalo

    tp = min(ROW_TILE, t_)

    def proj(i, carry):
        r0 = pl.multiple_of(i * tp, tp)
        xp_s[pl.ds(pl.multiple_of(HALO + r0, HALO), tp), :] = _dot(h_ref[0, pl.ds(r0, tp), :], w_ref[:, 0:GROUP_W])
        return carry

    lax.fori_loop(0, t_ // tp, proj, 0)

    lam = lam_ref[...]
    sp = jnp.maximum(-lam, 0.0) + jnp.log(1.0 + jnp.exp(-jnp.abs(lam)))
    nsp = -LRU_C * sp

    def gates(i, carry):
        r0 = pl.multiple_of(i * tr, tr)
        rows = pl.ds(r0, tr)
        win = xp_s[pl.ds(r0, tr + 2 * HALO), :]
        xc = cb_ref[...]
        for j in range(CONV_W):
            st = HALO + j - CONV_LEFT
            xc = xc + cw_ref[j:j + 1, :] * win[st:st + tr, :]
        g = _dot(xc.astype(BF16), gw_ref[...]) + gb_ref[...]
        for d in range(2):
            r = _sigmoid(g[:, (2 * d) * GROUP_W:(2 * d + 1) * GROUP_W])
            ig = _sigmoid(g[:, (2 * d + 1) * GROUP_W:(2 * d + 2) * GROUP_W])
            a = jnp.exp(r * nsp[d:d + 1, :])
            a_s[d, rows, :] = a
            b_s[d, rows, :] = jnp.sqrt(1.0 - a * a) * (ig * xc)
        return carry

    lax.fori_loop(0, t_ // tr, gates, 0)

    nt = t_ // 8

    def tile(i, carry):
        cf, cr = carry
        rf = pl.multiple_of(i * 8, 8)
        rr = pl.multiple_of((nt - 1 - i) * 8, 8)
        af, bf = _tile_scan(a_s[0, pl.ds(rf, 8), :], b_s[0, pl.ds(rf, 8), :], False)
        ar, br = _tile_scan(a_s[1, pl.ds(rr, 8), :], b_s[1, pl.ds(rr, 8), :], True)
        hf = bf + af * cf
        hr = br + ar * cr
        hs_s[0, pl.ds(rf, 8), :] = hf
        hs_s[1, pl.ds(rr, 8), :] = hr
        return hf[7:8, :], hr[0:1, :]

    z = jnp.zeros((1, GROUP_W), F32)
    lax.fori_loop(0, nt, tile, (z, z), unroll=LRU_UNROLL if nt % LRU_UNROLL == 0 else 1)

    def epilogue(i, carry):
        r0 = pl.multiple_of(i * tr, tr)
        rows = pl.ds(r0, tr)
        gate = _dot(h_ref[0, rows, :], w_ref[:, GROUP_W:])
        o_ref[0, rows, :] = ((hs_s[0, rows, :] + hs_s[1, rows, :]) * _silu(gate)).astype(BF16)
        return carry

    lax.fori_loop(0, t_ // tr, epilogue, 0)


def _lru(h3, w, cw, cb, gw, gb, lam):
    b_, t_, _ = h3.shape
    return pl.pallas_call(
        functools.partial(_lru_kernel, t_=t_, tr=min(256, t_)),
        grid=(b_,),
        in_specs=[pl.BlockSpec((1, t_, D_MODEL), lambda b: (b, 0, 0)), _full(w), _full(cw), _full(cb),
                  _full(gw), _full(gb), _full(lam)],
        out_specs=pl.BlockSpec((1, t_, GROUP_W), lambda b: (b, 0, 0)),
        out_shape=jax.ShapeDtypeStruct((b_, t_, GROUP_W), BF16),
        scratch_shapes=[pltpu.VMEM((t_ + 2 * HALO, GROUP_W), F32),
                        pltpu.VMEM((2, t_, GROUP_W), F32), pltpu.VMEM((2, t_, GROUP_W), F32),
                        pltpu.VMEM((2, t_, GROUP_W), F32)],
        compiler_params=_params(),
        name="lru",
    )(h3, w, cw, cb, gw, gb, lam)


def _rwkv_tables():
    bd = _head_blockdiag()
    t = np.arange(CHUNK)
    tri = np.stack([(t[:, None] >= t[None, :]), (t[:, None] <= t[None, :])]).astype(np.float32)
    strict = np.stack([(t[:, None] > t[None, :]), (t[:, None] < t[None, :])]).astype(np.float32)
    incl = tri.copy()
    eye = np.eye(CHUNK, dtype=np.float32)
    wide = lambda m: np.tile(m, (1,) * (m.ndim - 1) + (N_HEADS_G,))
    return dict(bd=bd, bdb=bd, tri2=np.concatenate([tri, tri], axis=-1), strict=wide(strict), incl=wide(incl),
                eye=wide(eye))


def _rwkv_kernel(h_ref, w_ref, sh_ref, w0_ref, wup_ref, a0_ref, aup_ref, kk_ref, ka_ref, rk_ref,
                 lng_ref, lnb_ref, bd_ref, bdb_ref, tri_ref, strict_ref, incl_ref, eye_ref, o_ref,
                 zs_s, r_s, v_s, lg_s, pt_s, qd_s, kd_s, y_s, st_s, *, t_, tr):
    bd = bd_ref[...]
    bdb = bdb_ref[...]
    g = GROUP_W
    sw = RWKV_SHIFT_W
    cc = CHUNK
    bf = lambda x: x.astype(BF16)

    zhalo = jnp.zeros((HALO, sw), F32)
    zs_s[0:HALO, :] = zhalo
    zs_s[HALO + t_:, :] = zhalo

    tp = min(ROW_TILE, t_)

    def proj(i, carry):
        r0 = pl.multiple_of(i * tp, tp)
        zs_s[pl.ds(pl.multiple_of(HALO + r0, HALO), tp), :] = _dot(h_ref[0, pl.ds(r0, tp), :], w_ref[:, 0:sw])
        return carry

    lax.fori_loop(0, t_ // tp, proj, 0)

    def prep(i, carry):
        r0 = pl.multiple_of(i * tr, tr)
        rows = pl.ds(r0, tr)

        def mixed(c0, c1):
            win = zs_s[pl.ds(r0, tr + 2 * HALO), c0:c1]
            x = win[HALO:HALO + tr, :]
            prev = win[HALO - 1:HALO - 1 + tr, :]
            nxt = win[HALO + 1:HALO + 1 + tr, :]
            return x + sh_ref[0:1, c0:c1] * (prev - x) + sh_ref[1:2, c0:c1] * (nxt - x)

        r_s[rows, :] = mixed(0, g)
        v_s[rows, :] = mixed(2 * g, 3 * g)
        k = mixed(g, 2 * g)
        lo = mixed(3 * g, sw)
        kk = k * kk_ref[...]
        ssq = _dot(bf(kk * kk), bdb)
        kk = kk / jnp.maximum(jnp.sqrt(ssq), 1e-12)
        dw = _dot(bf(jnp.tanh(lo)), wup_ref[...])
        da = _dot(bf(lo), aup_ref[...])
        for d in range(2):
            lw = -EXP_M05 * _sigmoid(w0_ref[d:d + 1, :] + dw[:, d * g:(d + 1) * g])
            a = _sigmoid(a0_ref[d:d + 1, :] + da[:, d * g:(d + 1) * g])
            kd_s[d, rows, :] = k * (1.0 + (a - 1.0) * ka_ref[...])
            qd_s[d, rows, :] = kk * a
            lw_hi = bf(lw)
            lw_lo = bf(lw - lw_hi.astype(F32))
            lgs = [_dot(tri_ref[d], jnp.concatenate([lw_hi[c0:c0 + cc, :], lw_lo[c0:c0 + cc, :]], axis=0))
                   for c0 in range(0, tr, cc)]
            lg = jnp.concatenate(lgs, axis=0) if len(lgs) > 1 else lgs[0]
            lg_s[d, rows, :] = lg
            pt_s[d, rows, :] = kk * jnp.exp(lg - lw)
        return carry

    lax.fori_loop(0, t_ // tr, prep, 0)
    st_s[...] = jnp.zeros((2, g, g), F32)

    nc = t_ // cc
    ug = min(RWKV_GROUP, nc)
    rep = lambda x: jnp.concatenate([bf(x)] * N_HEADS_G, axis=0) * bdb
    stack = lambda a, b: bf(jnp.concatenate([a, b], axis=0))
    eye = eye_ref[...]

    def group(i, carry):
        ch = []
        for j in range(ug):
            for d in range(2):
                c = i * ug + j
                c = c if d == 0 else nc - 1 - c
                rows = pl.ds(pl.multiple_of(c * cc, cc), cc)
                lg = lg_s[d, rows, :]
                tot = lg[cc - 1:cc, :] if d == 0 else lg[0:1, :]
                g_inv = jnp.exp(-lg)
                g_end = jnp.exp(tot - lg)
                qdc = qd_s[d, rows, :]
                kdc = kd_s[d, rows, :]
                ch.append(dict(d=d, rows=rows, gtot=jnp.exp(tot), pt=pt_s[d, rows, :],
                               rt=r_s[rows, :] * jnp.exp(lg), qt=qdc * g_inv, kt=kdc * g_inv,
                               qe=qdc * g_end, ke=kdc * g_end, vc=v_s[rows, :]))
        for s in ch:
            pr = stack(s["pt"], s["rt"])
            a_q = _dot_nt(pr, rep(s["qt"]))
            a_k = _dot_nt(pr, rep(s["kt"]))
            strict = strict_ref[s["d"]]
            incl = incl_ref[s["d"]]
            s["a_rq"] = bf(a_q[cc:, :] * incl)
            s["a_k"] = stack(a_k[:cc, :] * strict, a_k[cc:, :] * incl)
            s["x"] = -(a_q[:cc, :] * strict)
            s["tm"] = eye + s["x"]
        for s in ch:
            s["x"] = _dot(bf(s["x"]), rep(s["x"]))
        for k in range(5):
            for s in ch:
                if k < 4:
                    both = _dot(stack(s["x"], s["tm"]), rep(s["x"]))
                    s["x"] = both[:cc, :]
                    s["tm"] = s["tm"] + both[cc:, :]
                else:
                    s["tm"] = s["tm"] + _dot(bf(s["tm"]), rep(s["x"]))
        for s in ch:
            s["tmb"] = bf(s["tm"])
            s["wm"] = _dot(s["tmb"], rep(s["pt"]))
            s["av"] = _dot(s["a_k"], rep(s["vc"]))
        for s in ch:
            s["u0"] = _dot(s["tmb"], rep(s["av"][:cc, :]))
            s["rm"] = bf(s["rt"] - _dot(s["a_rq"], rep(s["wm"])))
            s["mm"] = bf(_dot_tn(s["wm"], s["qe"]) * bd)
        for s in ch:
            s["y0"] = s["av"][cc:, :] - _dot(s["a_rq"], rep(s["u0"]))
            s["nn"] = _dot_tn(jnp.concatenate([s["vc"], -s["u0"]], axis=0),
                              jnp.concatenate([s["ke"], s["qe"]], axis=0)) * bd
        sts = [st_s[0], st_s[1]]
        for j in range(ug):
            for d in range(2):
                s = ch[2 * j + d]
                sb = bf(sts[d])
                y_s[d, s["rows"], :] = _dot_nt(s["rm"], sb) + s["y0"]
                sts[d] = sts[d] * s["gtot"] - _dot(sb, s["mm"]) + s["nn"]
        st_s[0] = sts[0]
        st_s[1] = sts[1]
        return carry

    lax.fori_loop(0, nc // ug, group, 0)

    def epilogue(i, carry):
        r0 = pl.multiple_of(i * tr, tr)
        rows = pl.ds(r0, tr)
        y = y_s[0, rows, :] + y_s[1, rows, :]
        mu = _dot(bf(y), bdb) * (1.0 / HEAD_DIM)
        yc = y - mu
        var = _dot(bf(yc * yc), bdb) * (1.0 / HEAD_DIM)
        ksum = kd_s[0, rows, :] + kd_s[1, rows, :]
        bonus = _dot(bf(r_s[rows, :] * ksum * rk_ref[...]), bdb) * v_s[rows, :]
        yn = yc * lax.rsqrt(var + GN_EPS) * lng_ref[...] + lnb_ref[...] + bonus
        gate = _dot(h_ref[0, rows, :], w_ref[:, sw:])
        o_ref[0, rows, :] = (yn * _silu(gate)).astype(BF16)
        return carry

    lax.fori_loop(0, t_ // tr, epilogue, 0)


def _rwkv(h3, w, p, tabs):
    b_, t_, _ = h3.shape
    args = [h3, w, p["shift"], p["w0"], p["wup"], p["a0"], p["aup"], p["kk"], p["ka"], p["rk"], p["lng"],
            p["lnb"], tabs["bd"], tabs["bdb"], tabs["tri2"], tabs["strict"], tabs["incl"], tabs["eye"]]
    big = lambda: pltpu.VMEM((t_, GROUP_W), F32)
    big2 = lambda: pltpu.VMEM((2, t_, GROUP_W), F32)
    return pl.pallas_call(
        functools.partial(_rwkv_kernel, t_=t_, tr=min(128, t_)),
        grid=(b_,),
        in_specs=[pl.BlockSpec((1, t_, D_MODEL), lambda b: (b, 0, 0))] + [_full(a) for a in args[1:]],
        out_specs=pl.BlockSpec((1, t_, GROUP_W), lambda b: (b, 0, 0)),
        out_shape=jax.ShapeDtypeStruct((b_, t_, GROUP_W), BF16),
        scratch_shapes=[pltpu.VMEM((t_ + 2 * HALO, RWKV_SHIFT_W), F32),
                        big(), big(), big2(), big2(), big2(), big2(), big2(),
                        pltpu.VMEM((2, GROUP_W, GROUP_W), F32)],
        compiler_params=_params(),
        name="rwkv",
    )(*args)


def _prep_layer(l, P):
    w_in = P["w_in"][l].astype(BF16)
    c0, c1, c2 = A_W, A_W + B_W, A_W + B_W + C_W
    pad = jnp.zeros((2, RWKV_DECAY_RANK, GROUP_W), F32)
    wup = jnp.concatenate([P["rwkv_w_up"][l], pad], axis=1)
    aup = jnp.concatenate([pad, P["rwkv_a_up"][l]], axis=1)
    cat2 = lambda m: jnp.concatenate([m[0], m[1]], axis=1).astype(BF16)
    gw = P["lru_gate_w"][l]
    blocks = []
    for d in range(2):
        for kk in range(2):
            blocks.append(jax.scipy.linalg.block_diag(*[gw[d, kk, n] for n in range(LRU_BLOCKS)]))
    row = lambda a: a.reshape(1, -1)
    return dict(
        wA=w_in[:, :c0], wB=w_in[:, c0:c1], wC=w_in[:, c1:c2], wD=w_in[:, c2:],
        w_out=P["w_out"][l].astype(BF16),
        rwkv=dict(shift=P["rwkv_shift"][l], w0=P["rwkv_w0"][l], wup=cat2(wup), a0=P["rwkv_a0"][l],
                  aup=cat2(aup), kk=row(P["rwkv_k_k"][l]), ka=row(P["rwkv_k_a"][l]),
                  rk=row(P["rwkv_r_k"][l]), lng=row(P["rwkv_ln_g"][l]), lnb=row(P["rwkv_ln_b"][l])),
        qw=jnp.tile(P["attn_q_norm"][l], 2).reshape(1, -1), kw=jnp.tile(P["attn_k_norm"][l], 2).reshape(1, -1),
        cw=P["lru_conv_w"][l], cb=row(P["lru_conv_b"][l]),
        gw=jnp.concatenate(blocks, axis=1).astype(BF16), gb=P["lru_gate_b"][l].reshape(1, -1),
        lam=P["lru_lambda"][l], sink=P["swa_sink"][l],
    )


def _tables(t_):
    cos, sin = _rope_tables(t_)
    qi = np.arange(Q_BLOCK)[:, None]
    ki = np.arange(3 * Q_BLOCK)[None, :]
    dist = np.abs(ki - WINDOW - qi).astype(np.float32)
    slopes = np.exp2(-8.0 * np.arange(1, N_HEADS_G + 1, dtype=np.float32) / N_HEADS_G).astype(np.float32)
    alibi = np.where(dist[None] <= WINDOW, -slopes[:, None, None] * dist[None], NEG).astype(np.float32)
    first = np.where(ki[None] < Q_BLOCK, NEG, alibi).astype(np.float32)
    last = np.where(ki[None] >= 2 * Q_BLOCK, NEG, alibi).astype(np.float32)
    wbias = np.ascontiguousarray(np.stack([alibi, first, last]).transpose(0, 1, 3, 2))
    tabs = dict(cos=cos, sin=sin, seg128=_head_blockdiag()[:128, :128], wbias=wbias)
    tabs.update(_rwkv_tables())
    out = {k: jnp.asarray(v) for k, v in tabs.items()}
    for k in ("bdb", "tri2", "seg128"):
        out[k] = out[k].astype(BF16)
    return out


def _trunk(x, P, layers):
    b_, t_, _ = x.shape
    tabs = _tables(t_)
    x2 = x.reshape(b_ * t_, D_MODEL)
    h2 = _norm(x2, P["norm_g"][0].reshape(1, -1))
    for l in range(DEPTH):
        lp = layers[l]
        h3 = h2.reshape(b_, t_, D_MODEL)
        oa = _rwkv(h3, lp["wA"], lp["rwkv"], tabs)
        ob = _gattn(h3, lp["wB"], lp["qw"], lp["kw"], tabs)
        oc = _lru(h3, lp["wC"], lp["cw"], lp["cb"], lp["gw"], lp["gb"], lp["lam"])
        od = _wattn(h3, lp["wD"], lp["sink"], tabs)
        flat = lambda o: o.reshape(b_ * t_, GROUP_W)
        final = l == DEPTH - 1
        g_next = (P["final_g"] if final else P["norm_g"][l + 1]).reshape(1, -1)
        outs = _out_proj(flat(oa), flat(ob), flat(oc), flat(od), lp["w_out"], x2, g_next, final)
        if final:
            x2 = outs[0]
        else:
            x2, h2 = outs
    return x2.reshape(b_, t_, D_MODEL)


def kernel(x_prompt, x_sample, norm_g, w_in, w_out, rwkv_shift, rwkv_w0, rwkv_w_up, rwkv_a0, rwkv_a_up,
           rwkv_k_k, rwkv_k_a, rwkv_r_k, rwkv_ln_g, rwkv_ln_b, attn_q_norm, attn_k_norm, lru_conv_w,
           lru_conv_b, lru_gate_w, lru_gate_b, lru_lambda, swa_sink, final_g):
    P = dict(norm_g=norm_g, w_in=w_in, w_out=w_out, rwkv_shift=rwkv_shift, rwkv_w0=rwkv_w0,
             rwkv_w_up=rwkv_w_up, rwkv_a0=rwkv_a0, rwkv_a_up=rwkv_a_up, rwkv_k_k=rwkv_k_k,
             rwkv_k_a=rwkv_k_a, rwkv_r_k=rwkv_r_k, rwkv_ln_g=rwkv_ln_g, rwkv_ln_b=rwkv_ln_b,
             attn_q_norm=attn_q_norm, attn_k_norm=attn_k_norm, lru_conv_w=lru_conv_w,
             lru_conv_b=lru_conv_b, lru_gate_w=lru_gate_w, lru_gate_b=lru_gate_b,
             lru_lambda=lru_lambda, swa_sink=swa_sink, final_g=final_g)
    layers = [_prep_layer(l, P) for l in range(DEPTH)]
    return _trunk(x_prompt, P, layers), _trunk(x_sample, P, layers)
```

```python
import functools
import math

import jax
import jax.numpy as jnp
import numpy as np
from jax import lax
from jax.experimental import pallas as pl
from jax.experimental.pallas import tpu as pltpu

D_MODEL = 1024
DEPTH = 4
GRID_W = 64
HEAD_DIM = 64
GROUP_W = 256
N_HEADS_G = 4
N_KV = 2
REP = 2
KV_W = 128
RWKV_DECAY_RANK = 64
RWKV_SHIFT_W = 896
LRU_C = 8.0
LRU_BLOCKS = 4
LRU_BLK = 64
CONV_W = 4
CONV_LEFT = 2
Q_BLOCK = 128
WINDOW = 128
ROPE_THETA = 10000.0
NORM_EPS = 1e-6
GN_EPS = 64e-5
NEG = -1e30
A_W = 1152
B_W = 768
C_W = 512
D_W = 768

CHUNK = 64
RWKV_GROUP = 4
LRU_UNROLL = 4
WATTN_BLOCKS = 2
ROW_TILE = 512
OUT_TILE = 1024
VT_ROWS = 80
HALO = 8
V7X_VMEM_LIMIT = 56 * 1024 * 1024
F32 = jnp.float32
BF16 = jnp.bfloat16
HI = lax.Precision.HIGHEST
EXP_M05 = math.exp(-0.5)
LOG2E = math.log2(math.e)


def _dot(a, b, prec=None):
    return jnp.dot(a, b, preferred_element_type=F32, precision=prec)


def _dot_nt(a, b, prec=None):
    return lax.dot_general(a, b, (((1,), (1,)), ((), ())), preferred_element_type=F32, precision=prec)


def _dot_tn(a, b, prec=None):
    return lax.dot_general(a, b, (((0,), (0,)), ((), ())), preferred_element_type=F32, precision=prec)


def _sigmoid(x):
    return 1.0 / (1.0 + jnp.exp(-x))


def _silu(x):
    return x * _sigmoid(x)


def _rms(x, g):
    return x * lax.rsqrt(jnp.mean(x * x, axis=-1, keepdims=True) + NORM_EPS) * g


def _params(n_axes=1):
    return pltpu.CompilerParams(dimension_semantics=("arbitrary",) * n_axes,
                                vmem_limit_bytes=V7X_VMEM_LIMIT)


def _full(a):
    nd = a.ndim
    return pl.BlockSpec(a.shape, lambda *_: (0,) * nd)


def _norm_kernel(x_ref, g_ref, h_ref):
    h_ref[...] = _rms(x_ref[...], g_ref[...]).astype(BF16)


def _norm(x2, g):
    m = x2.shape[0]
    tm = min(ROW_TILE, m)
    return pl.pallas_call(
        _norm_kernel,
        grid=(m // tm,),
        in_specs=[pl.BlockSpec((tm, D_MODEL), lambda i: (i, 0)), _full(g)],
        out_specs=pl.BlockSpec((tm, D_MODEL), lambda i: (i, 0)),
        out_shape=jax.ShapeDtypeStruct((m, D_MODEL), BF16),
        compiler_params=_params(),
        name="norm",
    )(x2, g)


def _out_kernel(oa_ref, ob_ref, oc_ref, od_ref, w_ref, x_ref, g_ref, *out_refs, final):
    acc = _dot(oa_ref[...], w_ref[0:GROUP_W, :])
    acc += _dot(ob_ref[...], w_ref[GROUP_W:2 * GROUP_W, :])
    acc += _dot(oc_ref[...], w_ref[2 * GROUP_W:3 * GROUP_W, :])
    acc += _dot(od_ref[...], w_ref[3 * GROUP_W:4 * GROUP_W, :])
    xn = x_ref[...] + acc
    if final:
        out_refs[0][...] = _rms(xn, g_ref[...])
    else:
        out_refs[0][...] = xn
        out_refs[1][...] = _rms(xn, g_ref[...]).astype(BF16)


def _out_proj(oa, ob, oc, od, w, x2, g, final):
    m = x2.shape[0]
    tm = min(OUT_TILE, m)
    ospec = pl.BlockSpec((tm, GROUP_W), lambda i: (i, 0))
    xspec = pl.BlockSpec((tm, D_MODEL), lambda i: (i, 0))
    if final:
        out_shape = (jax.ShapeDtypeStruct((m, D_MODEL), F32),)
        out_specs = (xspec,)
    else:
        out_shape = (jax.ShapeDtypeStruct((m, D_MODEL), F32), jax.ShapeDtypeStruct((m, D_MODEL), BF16))
        out_specs = (xspec, xspec)
    return pl.pallas_call(
        functools.partial(_out_kernel, final=final),
        grid=(m // tm,),
        in_specs=[ospec, ospec, ospec, ospec, _full(w), xspec, _full(g)],
        out_specs=out_specs,
        out_shape=out_shape,
        compiler_params=_params(),
        name="out_proj",
    )(oa, ob, oc, od, w, x2, g)


def _head_blockdiag():
    i = np.arange(GROUP_W)
    return (i[:, None] // HEAD_DIM == i[None, :] // HEAD_DIM).astype(np.float32)


def _rope_tables(t_):
    rows = t_ // GRID_W
    row = np.repeat(np.arange(rows), GRID_W).astype(np.float32)
    col = np.tile(np.arange(GRID_W), rows).astype(np.float32)
    half = HEAD_DIM // 2
    inv = (ROPE_THETA ** (-np.arange(0, half, 2, dtype=np.float32) / half)).astype(np.float32)
    ang_r = row[:, None] * inv
    ang_c = col[:, None] * inv
    cos = np.concatenate([np.cos(ang_r), np.cos(ang_r), np.cos(ang_c), np.cos(ang_c)], -1)
    sin = np.concatenate([-np.sin(ang_r), np.sin(ang_r), -np.sin(ang_c), np.sin(ang_c)], -1)
    return (np.tile(cos, (1, 2)).astype(np.float32), np.tile(sin, (1, 2)).astype(np.float32))


def _swap16(x):
    n = x.shape[-1]
    up = pltpu.roll(x, n - 16, axis=1)
    dn = pltpu.roll(x, 16, axis=1)
    lane = lax.broadcasted_iota(jnp.int32, x.shape, 1)
    return jnp.where((lane % 32) < 16, up, dn)


def _place_heads(half0, half1):
    lo = lax.broadcasted_iota(jnp.int32, half0.shape, 1) < HEAD_DIM
    z = jnp.zeros_like(half0)
    return [jnp.where(lo, half0, z), jnp.where(lo, pltpu.roll(half0, HEAD_DIM, axis=1), z),
            jnp.where(lo, z, pltpu.roll(half1, HEAD_DIM, axis=1)), jnp.where(lo, z, half1)]


def _gattn_kernel(h_ref, w_ref, qw_ref, kw_ref, cos_ref, sin_ref, seg_ref, o_ref,
                  qh_s, k_s, vt_s, *, t_, tq, tr):
    seg = seg_ref[...]

    def prologue(i, carry):
        r0 = pl.multiple_of(i * tr, tr)
        rows = pl.ds(r0, tr)
        h = h_ref[0, rows, :]
        cos = cos_ref[rows, :]
        sin = sin_ref[rows, :]

        def normrope(z, wgt):
            ms = _dot((z * z).astype(BF16), seg) * (1.0 / HEAD_DIM)
            z = z * lax.rsqrt(ms + NORM_EPS) * wgt
            return z * cos + _swap16(z) * sin

        z = _dot(h, w_ref[:, 0:GROUP_W + 2 * KV_W])
        halves = [normrope(z[:, c * 128:(c + 1) * 128], qw_ref[...]) * (HEAD_DIM ** -0.5 * LOG2E)
                  for c in range(2)]
        for hd, qm in enumerate(_place_heads(*halves)):
            qh_s[hd, rows, :] = qm.astype(BF16)
        k_s[rows, :] = normrope(z[:, GROUP_W:GROUP_W + KV_W], kw_ref[...]).astype(BF16)
        vt = z[:, GROUP_W + KV_W:].T.astype(BF16)
        ones = jnp.ones((VT_ROWS - HEAD_DIM, tr), BF16)
        for kv in range(N_KV):
            vt_s[i, kv * VT_ROWS:kv * VT_ROWS + HEAD_DIM, :] = vt[kv * HEAD_DIM:(kv + 1) * HEAD_DIM, :]
            vt_s[i, kv * VT_ROWS + HEAD_DIM:(kv + 1) * VT_ROWS, :] = ones
        return carry

    lax.fori_loop(0, t_ // tr, prologue, 0)
    nvc = t_ // tr

    def qblock(i, carry):
        r0 = pl.multiple_of(i * tq, tq)
        rows = pl.ds(r0, tq)
        scores = lambda hd: _dot_nt(k_s[...], qh_s[hd, rows, :])
        outs = []
        pairs = [(0, 1), (2, 3)]
        s_next = [scores(hd) for hd in pairs[0]]
        for pi, pair in enumerate(pairs):
            ss = s_next
            if pi + 1 < len(pairs):
                s_next = [scores(hd) for hd in pairs[pi + 1]]
            st = [dict(m=jnp.full((1, tq), NEG, F32), ot=jnp.zeros((VT_ROWS, tq), F32)) for _ in pair]
            for c in range(nvc):
                for k, hd in enumerate(pair):
                    g0 = (hd // REP) * VT_ROWS
                    a = st[k]
                    sc = ss[k][c * tr:(c + 1) * tr, :]
                    mc = jnp.maximum(a["m"], jnp.max(sc, axis=0, keepdims=True))
                    alpha = jnp.exp2(a["m"] - mc)
                    p = jnp.exp2(sc - mc)
                    a["ot"] = a["ot"] * alpha + _dot(vt_s[c, g0:g0 + VT_ROWS, :], p.astype(BF16))
                    a["m"] = mc
            outs += [a["ot"][:HEAD_DIM, :] * (1.0 / a["ot"][HEAD_DIM:HEAD_DIM + 1, :]) for a in st]
        o = jnp.concatenate(outs, axis=0).T
        gate = _dot(h_ref[0, rows, :], w_ref[:, GROUP_W + 2 * KV_W:])
        o_ref[0, rows, :] = (o * _silu(gate)).astype(BF16)
        return carry

    lax.fori_loop(0, t_ // tq, qblock, 0)


def _gattn(h3, w, qw, kw, tabs):
    b_, t_, _ = h3.shape
    tq = min(256, t_)
    tr = min(ROW_TILE, t_)
    cos, sin, seg = tabs["cos"], tabs["sin"], tabs["seg128"]
    return pl.pallas_call(
        functools.partial(_gattn_kernel, t_=t_, tq=tq, tr=tr),
        grid=(b_,),
        in_specs=[pl.BlockSpec((1, t_, D_MODEL), lambda b: (b, 0, 0)), _full(w), _full(qw), _full(kw),
                  _full(cos), _full(sin), _full(seg)],
        out_specs=pl.BlockSpec((1, t_, GROUP_W), lambda b: (b, 0, 0)),
        out_shape=jax.ShapeDtypeStruct((b_, t_, GROUP_W), BF16),
        scratch_shapes=[pltpu.VMEM((N_HEADS_G, t_, KV_W), BF16),
                        pltpu.VMEM((t_, KV_W), BF16),
                        pltpu.VMEM((t_ // tr, N_KV * VT_ROWS, tr), BF16)],
        compiler_params=_params(),
        name="gattn",
    )(h3, w, qw, kw, cos, sin, seg)


def _wattn_kernel(sink_ref, h_ref, w_ref, bias_ref, o_ref, qh_s, kp_s, vt_s, *, t_, tr):
    nb = t_ // Q_BLOCK
    zblk = jnp.zeros((Q_BLOCK, KV_W), BF16)
    kp_s[0:WINDOW, :] = zblk
    kp_s[WINDOW + t_:, :] = zblk
    vt_s[0] = zblk
    vt_s[nb + 1] = zblk
    bpt = tr // Q_BLOCK

    def prologue(i, carry):
        r0 = pl.multiple_of(i * tr, tr)
        rows = pl.ds(r0, tr)
        h = h_ref[0, rows, :]
        z = _dot(h, w_ref[:, 0:GROUP_W + 2 * KV_W])
        zq = z[:, 0:GROUP_W] * (HEAD_DIM ** -0.5)
        for hd, qm in enumerate(_place_heads(zq[:, 0:KV_W], zq[:, KV_W:])):
            qh_s[hd, rows, :] = qm.astype(BF16)
        kp_s[pl.ds(pl.multiple_of(WINDOW + r0, WINDOW), tr), :] = z[:, GROUP_W:GROUP_W + KV_W].astype(BF16)
        zv = z[:, GROUP_W + KV_W:]
        for jb in range(bpt):
            vt_s[1 + i * bpt + jb] = zv[jb * Q_BLOCK:(jb + 1) * Q_BLOCK, :].T.astype(BF16)
        return carry

    lax.fori_loop(0, t_ // tr, prologue, 0)

    kw = 3 * Q_BLOCK
    npb = min(WATTN_BLOCKS, nb)

    def qblocks(i, carry):
        items = []
        for j in range(npb):
            blk = i * npb + j
            r0 = pl.multiple_of(blk * Q_BLOCK, Q_BLOCK)
            var = jnp.where(blk == 0, 1, jnp.where(blk == nb - 1, 2, 0))
            kwin = kp_s[pl.ds(r0, kw), :]
            for hd in range(N_HEADS_G):
                items.append((j, hd, blk, var, _dot_nt(kwin, qh_s[hd, pl.ds(r0, Q_BLOCK), :])))
        outs = [[None] * N_HEADS_G for _ in range(npb)]
        for j, hd, blk, var, s in items:
            s = s + bias_ref[var, hd]
            sk = sink_ref[hd]
            m = jnp.maximum(jnp.max(s, axis=0, keepdims=True), sk)
            p = jnp.exp(s - m)
            den = jnp.sum(p, axis=0, keepdims=True) + jnp.exp(sk - m)
            pb = p.astype(BF16)
            g0 = (hd // REP) * HEAD_DIM
            ot = _dot(vt_s[blk, g0:g0 + HEAD_DIM, :], pb[0:Q_BLOCK, :])
            for jj in range(1, 3):
                ot += _dot(vt_s[blk + jj, g0:g0 + HEAD_DIM, :], pb[jj * Q_BLOCK:(jj + 1) * Q_BLOCK, :])
            outs[j][hd] = ot * (1.0 / den)
        for j in range(npb):
            r0 = pl.multiple_of((i * npb + j) * Q_BLOCK, Q_BLOCK)
            o = jnp.concatenate(outs[j], axis=0).T
            gate = _dot(h_ref[0, pl.ds(r0, Q_BLOCK), :], w_ref[:, GROUP_W + 2 * KV_W:])
            o_ref[0, pl.ds(r0, Q_BLOCK), :] = (o * _silu(gate)).astype(BF16)
        return carry

    lax.fori_loop(0, nb // npb, qblocks, 0)


def _wattn(h3, w, sink, tabs):
    b_, t_, _ = h3.shape
    bias = tabs["wbias"]
    nb = t_ // Q_BLOCK
    assert nb >= 2
    return pl.pallas_call(
        functools.partial(_wattn_kernel, t_=t_, tr=min(ROW_TILE, t_)),
        grid=(b_,),
        in_specs=[pl.BlockSpec(memory_space=pltpu.SMEM),
                  pl.BlockSpec((1, t_, D_MODEL), lambda b: (b, 0, 0)), _full(w), _full(bias)],
        out_specs=pl.BlockSpec((1, t_, GROUP_W), lambda b: (b, 0, 0)),
        out_shape=jax.ShapeDtypeStruct((b_, t_, GROUP_W), BF16),
        scratch_shapes=[pltpu.VMEM((N_HEADS_G, t_, KV_W), BF16),
                        pltpu.VMEM((t_ + 2 * WINDOW, KV_W), BF16),
                        pltpu.VMEM((nb + 2, KV_W, Q_BLOCK), BF16)],
        compiler_params=_params(),
        name="wattn",
    )(sink, h3, w, bias)


def _tile_scan(a, b, rev):
    sub = lax.broadcasted_iota(jnp.int32, a.shape, 0)
    for s in (1, 2, 4):
        if rev:
            a_sh = pltpu.roll(a, 8 - s, axis=0)
            b_sh = pltpu.roll(b, 8 - s, axis=0)
            ok = sub < 8 - s
        else:
            a_sh = pltpu.roll(a, s, axis=0)
            b_sh = pltpu.roll(b, s, axis=0)
            ok = sub >= s
        b = jnp.where(ok, a * b_sh + b, b)
        a = jnp.where(ok, a * a_sh, a)
    return a, b


def _lru_kernel(h_ref, w_ref, cw_ref, cb_ref, gw_ref, gb_ref, lam_ref, o_ref, xp_s, a_s, b_s, hs_s,
                *, t_, tr):
    zhalo = jnp.zeros((HALO, GROUP_W), F32)
    xp_s[0:HALO, :] = zhalo
    xp_s[HALO + t_:, :] = zhalo

    tp = min(ROW_TILE, t_)

    def proj(i, carry):
        r0 = pl.multiple_of(i * tp, tp)
        xp_s[pl.ds(pl.multiple_of(HALO + r0, HALO), tp), :] = _dot(h_ref[0, pl.ds(r0, tp), :], w_ref[:, 0:GROUP_W])
        return carry

    lax.fori_loop(0, t_ // tp, proj, 0)

    lam = lam_ref[...]
    sp = jnp.maximum(-lam, 0.0) + jnp.log(1.0 + jnp.exp(-jnp.abs(lam)))
    nsp = -LRU_C * sp

    def gates(i, carry):
        r0 = pl.multiple_of(i * tr, tr)
        rows = pl.ds(r0, tr)
        win = xp_s[pl.ds(r0, tr + 2 * HALO), :]
        xc = cb_ref[...]
        nw = tr + 2 * HALO
        for j in range(CONV_W):
            off = j - CONV_LEFT
            tap = win if off == 0 else pltpu.roll(win, (-off) % nw, axis=0)
            xc = xc + cw_ref[j:j + 1, :] * tap[HALO:HALO + tr, :]
        g = _dot(xc.astype(BF16), gw_ref[...]) + gb_ref[...]
        for d in range(2):
            r = _sigmoid(g[:, (2 * d) * GROUP_W:(2 * d + 1) * GROUP_W])
            ig = _sigmoid(g[:, (2 * d + 1) * GROUP_W:(2 * d + 2) * GROUP_W])
            a = jnp.exp(r * nsp[d:d + 1, :])
            a_s[d, rows, :] = a
            b_s[d, rows, :] = jnp.sqrt(1.0 - a * a) * (ig * xc)
        return carry

    lax.fori_loop(0, t_ // tr, gates, 0)

    nt = t_ // 8

    def tile(i, carry):
        cf, cr = carry
        rf = pl.multiple_of(i * 8, 8)
        rr = pl.multiple_of((nt - 1 - i) * 8, 8)
        af, bf = _tile_scan(a_s[0, pl.ds(rf, 8), :], b_s[0, pl.ds(rf, 8), :], False)
        ar, br = _tile_scan(a_s[1, pl.ds(rr, 8), :], b_s[1, pl.ds(rr, 8), :], True)
        hf = bf + af * cf
        hr = br + ar * cr
        hs_s[0, pl.ds(rf, 8), :] = hf
        hs_s[1, pl.ds(rr, 8), :] = hr
        return hf[7:8, :], hr[0:1, :]

    z = jnp.zeros((1, GROUP_W), F32)
    lax.fori_loop(0, nt, tile, (z, z), unroll=LRU_UNROLL if nt % LRU_UNROLL == 0 else 1)

    def epilogue(i, carry):
        r0 = pl.multiple_of(i * tr, tr)
        rows = pl.ds(r0, tr)
        gate = _dot(h_ref[0, rows, :], w_ref[:, GROUP_W:])
        o_ref[0, rows, :] = ((hs_s[0, rows, :] + hs_s[1, rows, :]) * _silu(gate)).astype(BF16)
        return carry

    lax.fori_loop(0, t_ // tr, epilogue, 0)


def _lru(h3, w, cw, cb, gw, gb, lam):
    b_, t_, _ = h3.shape
    return pl.pallas_call(
        functools.partial(_lru_kernel, t_=t_, tr=min(256, t_)),
        grid=(b_,),
        in_specs=[pl.BlockSpec((1, t_, D_MODEL), lambda b: (b, 0, 0)), _full(w), _full(cw), _full(cb),
                  _full(gw), _full(gb), _full(lam)],
        out_specs=pl.BlockSpec((1, t_, GROUP_W), lambda b: (b, 0, 0)),
        out_shape=jax.ShapeDtypeStruct((b_, t_, GROUP_W), BF16),
        scratch_shapes=[pltpu.VMEM((t_ + 2 * HALO, GROUP_W), F32),
                        pltpu.VMEM((2, t_, GROUP_W), F32), pltpu.VMEM((2, t_, GROUP_W), F32),
                        pltpu.VMEM((2, t_, GROUP_W), F32)],
        compiler_params=_params(),
        name="lru",
    )(h3, w, cw, cb, gw, gb, lam)


def _rwkv_tables():
    bd = _head_blockdiag()
    t = np.arange(CHUNK)
    tri = np.stack([(t[:, None] >= t[None, :]), (t[:, None] <= t[None, :])]).astype(np.float32)
    strict = np.stack([(t[:, None] > t[None, :]), (t[:, None] < t[None, :])]).astype(np.float32)
    incl = tri.copy()
    eye = np.eye(CHUNK, dtype=np.float32)
    wide = lambda m: np.tile(m, (1,) * (m.ndim - 1) + (N_HEADS_G,))
    return dict(bd=bd, bdb=bd, tri2=np.concatenate([tri, tri], axis=-1), strict=wide(strict), incl=wide(incl),
                eye=wide(eye))


def _rwkv_kernel(h_ref, w_ref, sh_ref, w0_ref, wup_ref, a0_ref, aup_ref, kk_ref, ka_ref, rk_ref,
                 lng_ref, lnb_ref, bd_ref, bdb_ref, tri_ref, strict_ref, incl_ref, eye_ref, o_ref,
                 zs_s, r_s, v_s, lg_s, pt_s, qd_s, kd_s, y_s, st_s, *, t_, tr):
    bd = bd_ref[...]
    bdb = bdb_ref[...]
    g = GROUP_W
    sw = RWKV_SHIFT_W
    cc = CHUNK
    bf = lambda x: x.astype(BF16)

    zhalo = jnp.zeros((HALO, sw), F32)
    zs_s[0:HALO, :] = zhalo
    zs_s[HALO + t_:, :] = zhalo

    tp = min(ROW_TILE, t_)

    def proj(i, carry):
        r0 = pl.multiple_of(i * tp, tp)
        zs_s[pl.ds(pl.multiple_of(HALO + r0, HALO), tp), :] = _dot(h_ref[0, pl.ds(r0, tp), :], w_ref[:, 0:sw])
        return carry

    lax.fori_loop(0, t_ // tp, proj, 0)

    def prep(i, carry):
        r0 = pl.multiple_of(i * tr, tr)
        rows = pl.ds(r0, tr)

        def mixed(c0, c1):
            win = zs_s[pl.ds(r0, tr + 2 * HALO), c0:c1]
            nw = tr + 2 * HALO
            x = win[HALO:HALO + tr, :]
            prev = pltpu.roll(win, 1, axis=0)[HALO:HALO + tr, :]
            nxt = pltpu.roll(win, nw - 1, axis=0)[HALO:HALO + tr, :]
            return x + sh_ref[0:1, c0:c1] * (prev - x) + sh_ref[1:2, c0:c1] * (nxt - x)

        r_s[rows, :] = mixed(0, g)
        v_s[rows, :] = mixed(2 * g, 3 * g)
        k = mixed(g, 2 * g)
        lo = mixed(3 * g, sw)
        kk = k * kk_ref[...]
        ssq = _dot(bf(kk * kk), bdb)
        kk = kk / jnp.maximum(jnp.sqrt(ssq), 1e-12)
        dw = _dot(bf(jnp.tanh(lo)), wup_ref[...])
        da = _dot(bf(lo), aup_ref[...])
        for d in range(2):
            lw = -EXP_M05 * _sigmoid(w0_ref[d:d + 1, :] + dw[:, d * g:(d + 1) * g])
            a = _sigmoid(a0_ref[d:d + 1, :] + da[:, d * g:(d + 1) * g])
            kd_s[d, rows, :] = k * (1.0 + (a - 1.0) * ka_ref[...])
            qd_s[d, rows, :] = kk * a
            lw_hi = bf(lw)
            lw_lo = bf(lw - lw_hi.astype(F32))
            lgs = [_dot(tri_ref[d], jnp.concatenate([lw_hi[c0:c0 + cc, :], lw_lo[c0:c0 + cc, :]], axis=0))
                   for c0 in range(0, tr, cc)]
            lg = jnp.concatenate(lgs, axis=0) if len(lgs) > 1 else lgs[0]
            lg_s[d, rows, :] = lg
            pt_s[d, rows, :] = kk * jnp.exp(lg - lw)
        return carry

    lax.fori_loop(0, t_ // tr, prep, 0)
    st_s[...] = jnp.zeros((2, g, g), F32)

    nc = t_ // cc
    ug = min(RWKV_GROUP, nc)
    rep = lambda x: jnp.concatenate([bf(x)] * N_HEADS_G, axis=0) * bdb
    stack = lambda a, b: bf(jnp.concatenate([a, b], axis=0))
    eye = eye_ref[...]

    def group(i, carry):
        ch = []
        for j in range(ug):
            for d in range(2):
                c = i * ug + j
                c = c if d == 0 else nc - 1 - c
                rows = pl.ds(pl.multiple_of(c * cc, cc), cc)
                lg = lg_s[d, rows, :]
                tot = lg[cc - 1:cc, :] if d == 0 else lg[0:1, :]
                g_inv = jnp.exp(-lg)
                g_end = jnp.exp(tot - lg)
                qdc = qd_s[d, rows, :]
                kdc = kd_s[d, rows, :]
                ch.append(dict(d=d, rows=rows, gtot=jnp.exp(tot), pt=pt_s[d, rows, :],
                               rt=r_s[rows, :] * jnp.exp(lg), qt=qdc * g_inv, kt=kdc * g_inv,
                               qe=qdc * g_end, ke=kdc * g_end, vc=v_s[rows, :]))
        for s in ch:
            pr = stack(s["pt"], s["rt"])
            a_q = _dot_nt(pr, rep(s["qt"]))
            a_k = _dot_nt(pr, rep(s["kt"]))
            strict = strict_ref[s["d"]]
            incl = incl_ref[s["d"]]
            s["a_rq"] = bf(a_q[cc:, :] * incl)
            s["a_k"] = stack(a_k[:cc, :] * strict, a_k[cc:, :] * incl)
            s["x"] = -(a_q[:cc, :] * strict)
            s["tm"] = eye + s["x"]
        for s in ch:
            s["x"] = _dot(bf(s["x"]), rep(s["x"]))
        for k in range(5):
            for s in ch:
                if k < 4:
                    both = _dot(stack(s["x"], s["tm"]), rep(s["x"]))
                    s["x"] = both[:cc, :]
                    s["tm"] = s["tm"] + both[cc:, :]
                else:
                    s["tm"] = s["tm"] + _dot(bf(s["tm"]), rep(s["x"]))
        for s in ch:
            s["tmb"] = bf(s["tm"])
            s["wm"] = _dot(s["tmb"], rep(s["pt"]))
            s["av"] = _dot(s["a_k"], rep(s["vc"]))
        for s in ch:
            s["u0"] = _dot(s["tmb"], rep(s["av"][:cc, :]))
            s["rm"] = bf(s["rt"] - _dot(s["a_rq"], rep(s["wm"])))
            s["mm"] = bf(_dot_tn(s["wm"], s["qe"]) * bd)
        for s in ch:
            s["y0"] = s["av"][cc:, :] - _dot(s["a_rq"], rep(s["u0"]))
            s["nn"] = _dot_tn(jnp.concatenate([s["vc"], -s["u0"]], axis=0),
                              jnp.concatenate([s["ke"], s["qe"]], axis=0)) * bd
        sts = [st_s[0], st_s[1]]
        for j in range(ug):
            for d in range(2):
                s = ch[2 * j + d]
                sb = bf(sts[d])
                y_s[d, s["rows"], :] = _dot_nt(s["rm"], sb) + s["y0"]
                sts[d] = sts[d] * s["gtot"] - _dot(sb, s["mm"]) + s["nn"]
        st_s[0] = sts[0]
        st_s[1] = sts[1]
        return carry

    lax.fori_loop(0, nc // ug, group, 0)

    def epilogue(i, carry):
        r0 = pl.multiple_of(i * tp, tp)
        rows = pl.ds(r0, tp)
        y = y_s[0, rows, :] + y_s[1, rows, :]
        mu = _dot(bf(y), bdb) * (1.0 / HEAD_DIM)
        yc = y - mu
        var = _dot(bf(yc * yc), bdb) * (1.0 / HEAD_DIM)
        ksum = kd_s[0, rows, :] + kd_s[1, rows, :]
        bonus = _dot(bf(r_s[rows, :] * ksum * rk_ref[...]), bdb) * v_s[rows, :]
        yn = yc * lax.rsqrt(var + GN_EPS) * lng_ref[...] + lnb_ref[...] + bonus
        gate = _dot(h_ref[0, rows, :], w_ref[:, sw:])
        o_ref[0, rows, :] = (yn * _silu(gate)).astype(BF16)
        return carry

    lax.fori_loop(0, t_ // tp, epilogue, 0)


def _rwkv(h3, w, p, tabs):
    b_, t_, _ = h3.shape
    args = [h3, w, p["shift"], p["w0"], p["wup"], p["a0"], p["aup"], p["kk"], p["ka"], p["rk"], p["lng"],
            p["lnb"], tabs["bd"], tabs["bdb"], tabs["tri2"], tabs["strict"], tabs["incl"], tabs["eye"]]
    big = lambda: pltpu.VMEM((t_, GROUP_W), F32)
    big2 = lambda: pltpu.VMEM((2, t_, GROUP_W), F32)
    return pl.pallas_call(
        functools.partial(_rwkv_kernel, t_=t_, tr=min(128, t_)),
        grid=(b_,),
        in_specs=[pl.BlockSpec((1, t_, D_MODEL), lambda b: (b, 0, 0))] + [_full(a) for a in args[1:]],
        out_specs=pl.BlockSpec((1, t_, GROUP_W), lambda b: (b, 0, 0)),
        out_shape=jax.ShapeDtypeStruct((b_, t_, GROUP_W), BF16),
        scratch_shapes=[pltpu.VMEM((t_ + 2 * HALO, RWKV_SHIFT_W), F32),
                        big(), big(), big2(), big2(), big2(), big2(), big2(),
                        pltpu.VMEM((2, GROUP_W, GROUP_W), F32)],
        compiler_params=_params(),
        name="rwkv",
    )(*args)


def _prep_layer(l, P):
    w_in = P["w_in"][l].astype(BF16)
    c0, c1, c2 = A_W, A_W + B_W, A_W + B_W + C_W
    pad = jnp.zeros((2, RWKV_DECAY_RANK, GROUP_W), F32)
    wup = jnp.concatenate([P["rwkv_w_up"][l], pad], axis=1)
    aup = jnp.concatenate([pad, P["rwkv_a_up"][l]], axis=1)
    cat2 = lambda m: jnp.concatenate([m[0], m[1]], axis=1).astype(BF16)
    gw = P["lru_gate_w"][l]
    blocks = []
    for d in range(2):
        for kk in range(2):
            blocks.append(jax.scipy.linalg.block_diag(*[gw[d, kk, n] for n in range(LRU_BLOCKS)]))
    row = lambda a: a.reshape(1, -1)
    return dict(
        wA=w_in[:, :c0], wB=w_in[:, c0:c1], wC=w_in[:, c1:c2], wD=w_in[:, c2:],
        w_out=P["w_out"][l].astype(BF16),
        rwkv=dict(shift=P["rwkv_shift"][l], w0=P["rwkv_w0"][l], wup=cat2(wup), a0=P["rwkv_a0"][l],
                  aup=cat2(aup), kk=row(P["rwkv_k_k"][l]), ka=row(P["rwkv_k_a"][l]),
                  rk=row(P["rwkv_r_k"][l]), lng=row(P["rwkv_ln_g"][l]), lnb=row(P["rwkv_ln_b"][l])),
        qw=jnp.tile(P["attn_q_norm"][l], 2).reshape(1, -1), kw=jnp.tile(P["attn_k_norm"][l], 2).reshape(1, -1),
        cw=P["lru_conv_w"][l], cb=row(P["lru_conv_b"][l]),
        gw=jnp.concatenate(blocks, axis=1).astype(BF16), gb=P["lru_gate_b"][l].reshape(1, -1),
        lam=P["lru_lambda"][l], sink=P["swa_sink"][l],
    )


def _tables(t_):
    cos, sin = _rope_tables(t_)
    qi = np.arange(Q_BLOCK)[:, None]
    ki = np.arange(3 * Q_BLOCK)[None, :]
    dist = np.abs(ki - WINDOW - qi).astype(np.float32)
    slopes = np.exp2(-8.0 * np.arange(1, N_HEADS_G + 1, dtype=np.float32) / N_HEADS_G).astype(np.float32)
    alibi = np.where(dist[None] <= WINDOW, -slopes[:, None, None] * dist[None], NEG).astype(np.float32)
    first = np.where(ki[None] < Q_BLOCK, NEG, alibi).astype(np.float32)
    last = np.where(ki[None] >= 2 * Q_BLOCK, NEG, alibi).astype(np.float32)
    wbias = np.ascontiguousarray(np.stack([alibi, first, last]).transpose(0, 1, 3, 2))
    tabs = dict(cos=cos, sin=sin, seg128=_head_blockdiag()[:128, :128], wbias=wbias)
    tabs.update(_rwkv_tables())
    out = {k: jnp.asarray(v) for k, v in tabs.items()}
    for k in ("bdb", "tri2", "seg128"):
        out[k] = out[k].astype(BF16)
    return out


def _trunk(x, P, layers):
    b_, t_, _ = x.shape
    tabs = _tables(t_)
    x2 = x.reshape(b_ * t_, D_MODEL)
    h2 = _norm(x2, P["norm_g"][0].reshape(1, -1))
    for l in range(DEPTH):
        lp = layers[l]
        h3 = h2.reshape(b_, t_, D_MODEL)
        oa = _rwkv(h3, lp["wA"], lp["rwkv"], tabs)
        ob = _gattn(h3, lp["wB"], lp["qw"], lp["kw"], tabs)
        oc = _lru(h3, lp["wC"], lp["cw"], lp["cb"], lp["gw"], lp["gb"], lp["lam"])
        od = _wattn(h3, lp["wD"], lp["sink"], tabs)
        flat = lambda o: o.reshape(b_ * t_, GROUP_W)
        final = l == DEPTH - 1
        g_next = (P["final_g"] if final else P["norm_g"][l + 1]).reshape(1, -1)
        outs = _out_proj(flat(oa), flat(ob), flat(oc), flat(od), lp["w_out"], x2, g_next, final)
        if final:
            x2 = outs[0]
        else:
            x2, h2 = outs
    return x2.reshape(b_, t_, D_MODEL)


def kernel(x_prompt, x_sample, norm_g, w_in, w_out, rwkv_shift, rwkv_w0, rwkv_w_up, rwkv_a0, rwkv_a_up,
           rwkv_k_k, rwkv_k_a, rwkv_r_k, rwkv_ln_g, rwkv_ln_b, attn_q_norm, attn_k_norm, lru_conv_w,
           lru_conv_b, lru_gate_w, lru_gate_b, lru_lambda, swa_sink, final_g):
    P = dict(norm_g=norm_g, w_in=w_in, w_out=w_out, rwkv_shift=rwkv_shift, rwkv_w0=rwkv_w0,
             rwkv_w_up=rwkv_w_up, rwkv_a0=rwkv_a0, rwkv_a_up=rwkv_a_up, rwkv_k_k=rwkv_k_k,
             rwkv_k_a=rwkv_k_a, rwkv_r_k=rwkv_r_k, rwkv_ln_g=rwkv_ln_g, rwkv_ln_b=rwkv_ln_b,
             attn_q_norm=attn_q_norm, attn_k_norm=attn_k_norm, lru_conv_w=lru_conv_w,
             lru_conv_b=lru_conv_b, lru_gate_w=lru_gate_w, lru_gate_b=lru_gate_b,
             lru_lambda=lru_lambda, swa_sink=swa_sink, final_g=final_g)
    layers = [_prep_layer(l, P) for l in range(DEPTH)]
    return _trunk(x_prompt, P, layers), _trunk(x_sample, P, layers)
```

```python
import functools
import math

import jax
import jax.numpy as jnp
import numpy as np
from jax import lax
from jax.experimental import pallas as pl
from jax.experimental.pallas import tpu as pltpu

D_MODEL = 1024
DEPTH = 4
GRID_W = 64
HEAD_DIM = 64
GROUP_W = 256
N_HEADS_G = 4
N_KV = 2
REP = 2
KV_W = 128
RWKV_DECAY_RANK = 64
RWKV_SHIFT_W = 896
LRU_C = 8.0
LRU_BLOCKS = 4
LRU_BLK = 64
CONV_W = 4
CONV_LEFT = 2
Q_BLOCK = 128
WINDOW = 128
ROPE_THETA = 10000.0
NORM_EPS = 1e-6
GN_EPS = 64e-5
NEG = -1e30
A_W = 1152
B_W = 768
C_W = 512
D_W = 768

CHUNK = 64
RWKV_GROUP = 4
LRU_UNROLL = 4
GATTN_BLOCKS = 2
WATTN_BLOCKS = 8
ROW_TILE = 512
OUT_TILE = 1024
VT_ROWS = 80
HALO = 8
V7X_VMEM_LIMIT = 56 * 1024 * 1024
F32 = jnp.float32
BF16 = jnp.bfloat16
HI = lax.Precision.HIGHEST
EXP_M05 = math.exp(-0.5)
LOG2E = math.log2(math.e)


def _dot(a, b, prec=None):
    return jnp.dot(a, b, preferred_element_type=F32, precision=prec)


def _dot_nt(a, b, prec=None):
    return lax.dot_general(a, b, (((1,), (1,)), ((), ())), preferred_element_type=F32, precision=prec)


def _dot_tn(a, b, prec=None):
    return lax.dot_general(a, b, (((0,), (0,)), ((), ())), preferred_element_type=F32, precision=prec)


def _sigmoid(x):
    return 1.0 / (1.0 + jnp.exp(-x))


def _silu(x):
    return x * _sigmoid(x)


def _rms(x, g):
    return x * lax.rsqrt(jnp.mean(x * x, axis=-1, keepdims=True) + NORM_EPS) * g


def _params(n_axes=1):
    return pltpu.CompilerParams(dimension_semantics=("arbitrary",) * n_axes,
                                vmem_limit_bytes=V7X_VMEM_LIMIT)


def _full(a):
    nd = a.ndim
    return pl.BlockSpec(a.shape, lambda *_: (0,) * nd)


def _norm_kernel(x_ref, g_ref, h_ref):
    h_ref[...] = _rms(x_ref[...], g_ref[...]).astype(BF16)


def _norm(x2, g):
    m = x2.shape[0]
    tm = min(ROW_TILE, m)
    return pl.pallas_call(
        _norm_kernel,
        grid=(m // tm,),
        in_specs=[pl.BlockSpec((tm, D_MODEL), lambda i: (i, 0)), _full(g)],
        out_specs=pl.BlockSpec((tm, D_MODEL), lambda i: (i, 0)),
        out_shape=jax.ShapeDtypeStruct((m, D_MODEL), BF16),
        compiler_params=_params(),
        name="norm",
    )(x2, g)


def _out_kernel(oa_ref, ob_ref, oc_ref, od_ref, w_ref, x_ref, g_ref, *out_refs, final):
    acc = _dot(oa_ref[...], w_ref[0:GROUP_W, :])
    acc += _dot(ob_ref[...], w_ref[GROUP_W:2 * GROUP_W, :])
    acc += _dot(oc_ref[...], w_ref[2 * GROUP_W:3 * GROUP_W, :])
    acc += _dot(od_ref[...], w_ref[3 * GROUP_W:4 * GROUP_W, :])
    xn = x_ref[...] + acc
    if final:
        out_refs[0][...] = _rms(xn, g_ref[...])
    else:
        out_refs[0][...] = xn
        out_refs[1][...] = _rms(xn, g_ref[...]).astype(BF16)


def _out_proj(oa, ob, oc, od, w, x2, g, final):
    m = x2.shape[0]
    tm = min(OUT_TILE, m)
    ospec = pl.BlockSpec((tm, GROUP_W), lambda i: (i, 0))
    xspec = pl.BlockSpec((tm, D_MODEL), lambda i: (i, 0))
    if final:
        out_shape = (jax.ShapeDtypeStruct((m, D_MODEL), F32),)
        out_specs = (xspec,)
    else:
        out_shape = (jax.ShapeDtypeStruct((m, D_MODEL), F32), jax.ShapeDtypeStruct((m, D_MODEL), BF16))
        out_specs = (xspec, xspec)
    return pl.pallas_call(
        functools.partial(_out_kernel, final=final),
        grid=(m // tm,),
        in_specs=[ospec, ospec, ospec, ospec, _full(w), xspec, _full(g)],
        out_specs=out_specs,
        out_shape=out_shape,
        compiler_params=_params(),
        name="out_proj",
    )(oa, ob, oc, od, w, x2, g)


def _head_blockdiag():
    i = np.arange(GROUP_W)
    return (i[:, None] // HEAD_DIM == i[None, :] // HEAD_DIM).astype(np.float32)


def _rope_tables(t_):
    rows = t_ // GRID_W
    row = np.repeat(np.arange(rows), GRID_W).astype(np.float32)
    col = np.tile(np.arange(GRID_W), rows).astype(np.float32)
    half = HEAD_DIM // 2
    inv = (ROPE_THETA ** (-np.arange(0, half, 2, dtype=np.float32) / half)).astype(np.float32)
    ang_r = row[:, None] * inv
    ang_c = col[:, None] * inv
    cos = np.concatenate([np.cos(ang_r), np.cos(ang_r), np.cos(ang_c), np.cos(ang_c)], -1)
    sin = np.concatenate([-np.sin(ang_r), np.sin(ang_r), -np.sin(ang_c), np.sin(ang_c)], -1)
    return (np.tile(cos, (1, 2)).astype(np.float32), np.tile(sin, (1, 2)).astype(np.float32))


def _swap16(x):
    n = x.shape[-1]
    up = pltpu.roll(x, n - 16, axis=1)
    dn = pltpu.roll(x, 16, axis=1)
    lane = lax.broadcasted_iota(jnp.int32, x.shape, 1)
    return jnp.where((lane % 32) < 16, up, dn)


def _place_heads(half0, half1):
    lo = lax.broadcasted_iota(jnp.int32, half0.shape, 1) < HEAD_DIM
    z = jnp.zeros_like(half0)
    return [jnp.where(lo, half0, z), jnp.where(lo, pltpu.roll(half0, HEAD_DIM, axis=1), z),
            jnp.where(lo, z, pltpu.roll(half1, HEAD_DIM, axis=1)), jnp.where(lo, z, half1)]


def _gattn_kernel(h_ref, w_ref, qw_ref, kw_ref, cos_ref, sin_ref, seg_ref, o_ref,
                  qh_s, k_s, vt_s, *, t_, tq, tr):
    seg = seg_ref[...]

    def prologue(i, carry):
        r0 = pl.multiple_of(i * tr, tr)
        rows = pl.ds(r0, tr)
        h = h_ref[0, rows, :]
        cos = cos_ref[rows, :]
        sin = sin_ref[rows, :]

        def normrope(z, wgt):
            ms = _dot((z * z).astype(BF16), seg) * (1.0 / HEAD_DIM)
            z = z * lax.rsqrt(ms + NORM_EPS) * wgt
            return z * cos + _swap16(z) * sin

        z = _dot(h, w_ref[:, 0:GROUP_W + 2 * KV_W])
        halves = [normrope(z[:, c * 128:(c + 1) * 128], qw_ref[...]) * (HEAD_DIM ** -0.5 * LOG2E)
                  for c in range(2)]
        for hd, qm in enumerate(_place_heads(*halves)):
            qh_s[hd, rows, :] = qm.astype(BF16)
        k_s[rows, :] = normrope(z[:, GROUP_W:GROUP_W + KV_W], kw_ref[...]).astype(BF16)
        vt = z[:, GROUP_W + KV_W:].T.astype(BF16)
        ones = jnp.ones((VT_ROWS - HEAD_DIM, tr), BF16)
        for kv in range(N_KV):
            vt_s[i, kv * VT_ROWS:kv * VT_ROWS + HEAD_DIM, :] = vt[kv * HEAD_DIM:(kv + 1) * HEAD_DIM, :]
            vt_s[i, kv * VT_ROWS + HEAD_DIM:(kv + 1) * VT_ROWS, :] = ones
        return carry

    lax.fori_loop(0, t_ // tr, prologue, 0)
    nvc = t_ // tr

    nqb = t_ // tq
    gb = min(GATTN_BLOCKS, nqb)

    def qblocks(i, carry):
        rows = [pl.ds(pl.multiple_of((i * gb + j) * tq, tq), tq) for j in range(gb)]
        units = [(j, pair) for j in range(gb) for pair in ((0, 1), (2, 3))]
        scores = lambda u: [_dot_nt(k_s[...], qh_s[hd, rows[u[0]], :]) for hd in u[1]]
        outs = [[] for _ in range(gb)]
        s_next = scores(units[0])
        for ui, (j, pair) in enumerate(units):
            ss = s_next
            if ui + 1 < len(units):
                s_next = scores(units[ui + 1])
            st = [dict(m=jnp.full((1, tq), NEG, F32), ot=jnp.zeros((VT_ROWS, tq), F32)) for _ in pair]
            for c in range(nvc):
                for k, hd in enumerate(pair):
                    g0 = (hd // REP) * VT_ROWS
                    a = st[k]
                    sc = ss[k][c * tr:(c + 1) * tr, :]
                    mc = jnp.maximum(a["m"], jnp.max(sc, axis=0, keepdims=True))
                    alpha = jnp.exp2(a["m"] - mc)
                    p = jnp.exp2(sc - mc)
                    a["ot"] = a["ot"] * alpha + _dot(vt_s[c, g0:g0 + VT_ROWS, :], p.astype(BF16))
                    a["m"] = mc
            outs[j] += [a["ot"][:HEAD_DIM, :] * (1.0 / a["ot"][HEAD_DIM:HEAD_DIM + 1, :]) for a in st]
        for j in range(gb):
            o = jnp.concatenate(outs[j], axis=0).T
            gate = _dot(h_ref[0, rows[j], :], w_ref[:, GROUP_W + 2 * KV_W:])
            o_ref[0, rows[j], :] = (o * _silu(gate)).astype(BF16)
        return carry

    lax.fori_loop(0, nqb // gb, qblocks, 0)


def _gattn(h3, w, qw, kw, tabs):
    b_, t_, _ = h3.shape
    tq = min(256, t_)
    tr = min(ROW_TILE, t_)
    cos, sin, seg = tabs["cos"], tabs["sin"], tabs["seg128"]
    return pl.pallas_call(
        functools.partial(_gattn_kernel, t_=t_, tq=tq, tr=tr),
        grid=(b_,),
        in_specs=[pl.BlockSpec((1, t_, D_MODEL), lambda b: (b, 0, 0)), _full(w), _full(qw), _full(kw),
                  _full(cos), _full(sin), _full(seg)],
        out_specs=pl.BlockSpec((1, t_, GROUP_W), lambda b: (b, 0, 0)),
        out_shape=jax.ShapeDtypeStruct((b_, t_, GROUP_W), BF16),
        scratch_shapes=[pltpu.VMEM((N_HEADS_G, t_, KV_W), BF16),
                        pltpu.VMEM((t_, KV_W), BF16),
                        pltpu.VMEM((t_ // tr, N_KV * VT_ROWS, tr), BF16)],
        compiler_params=_params(),
        name="gattn",
    )(h3, w, qw, kw, cos, sin, seg)


def _wattn_kernel(sink_ref, h_ref, w_ref, bias_ref, o_ref, qh_s, kp_s, vt_s, *, t_, tr):
    nb = t_ // Q_BLOCK
    zblk = jnp.zeros((Q_BLOCK, KV_W), BF16)
    kp_s[0:WINDOW, :] = zblk
    kp_s[WINDOW + t_:, :] = zblk
    vt_s[0] = zblk
    vt_s[nb + 1] = zblk
    bpt = tr // Q_BLOCK

    def prologue(i, carry):
        r0 = pl.multiple_of(i * tr, tr)
        rows = pl.ds(r0, tr)
        h = h_ref[0, rows, :]
        z = _dot(h, w_ref[:, 0:GROUP_W + 2 * KV_W])
        zq = z[:, 0:GROUP_W] * (HEAD_DIM ** -0.5)
        for hd, qm in enumerate(_place_heads(zq[:, 0:KV_W], zq[:, KV_W:])):
            qh_s[hd, rows, :] = qm.astype(BF16)
        kp_s[pl.ds(pl.multiple_of(WINDOW + r0, WINDOW), tr), :] = z[:, GROUP_W:GROUP_W + KV_W].astype(BF16)
        zv = z[:, GROUP_W + KV_W:]
        for jb in range(bpt):
            vt_s[1 + i * bpt + jb] = zv[jb * Q_BLOCK:(jb + 1) * Q_BLOCK, :].T.astype(BF16)
        return carry

    lax.fori_loop(0, t_ // tr, prologue, 0)

    kw = 3 * Q_BLOCK
    npb = min(WATTN_BLOCKS, nb)

    def qblocks(i, carry):
        items = []
        for j in range(npb):
            blk = i * npb + j
            r0 = pl.multiple_of(blk * Q_BLOCK, Q_BLOCK)
            var = jnp.where(blk == 0, 1, jnp.where(blk == nb - 1, 2, 0))
            kwin = kp_s[pl.ds(r0, kw), :]
            for hd in range(N_HEADS_G):
                items.append((j, hd, blk, var, _dot_nt(kwin, qh_s[hd, pl.ds(r0, Q_BLOCK), :])))
        outs = [[None] * N_HEADS_G for _ in range(npb)]
        for j, hd, blk, var, s in items:
            s = s + bias_ref[var, hd]
            sk = sink_ref[hd]
            m = jnp.maximum(jnp.max(s, axis=0, keepdims=True), sk)
            p = jnp.exp(s - m)
            den = jnp.sum(p, axis=0, keepdims=True) + jnp.exp(sk - m)
            pb = p.astype(BF16)
            g0 = (hd // REP) * HEAD_DIM
            ot = _dot(vt_s[blk, g0:g0 + HEAD_DIM, :], pb[0:Q_BLOCK, :])
            for jj in range(1, 3):
                ot += _dot(vt_s[blk + jj, g0:g0 + HEAD_DIM, :], pb[jj * Q_BLOCK:(jj + 1) * Q_BLOCK, :])
            outs[j][hd] = ot * (1.0 / den)
        for j in range(npb):
            r0 = pl.multiple_of((i * npb + j) * Q_BLOCK, Q_BLOCK)
            o = jnp.concatenate(outs[j], axis=0).T
            gate = _dot(h_ref[0, pl.ds(r0, Q_BLOCK), :], w_ref[:, GROUP_W + 2 * KV_W:])
            o_ref[0, pl.ds(r0, Q_BLOCK), :] = (o * _silu(gate)).astype(BF16)
        return carry

    lax.fori_loop(0, nb // npb, qblocks, 0)


def _wattn(h3, w, sink, tabs):
    b_, t_, _ = h3.shape
    bias = tabs["wbias"]
    nb = t_ // Q_BLOCK
    assert nb >= 2
    return pl.pallas_call(
        functools.partial(_wattn_kernel, t_=t_, tr=min(ROW_TILE, t_)),
        grid=(b_,),
        in_specs=[pl.BlockSpec(memory_space=pltpu.SMEM),
                  pl.BlockSpec((1, t_, D_MODEL), lambda b: (b, 0, 0)), _full(w), _full(bias)],
        out_specs=pl.BlockSpec((1, t_, GROUP_W), lambda b: (b, 0, 0)),
        out_shape=jax.ShapeDtypeStruct((b_, t_, GROUP_W), BF16),
        scratch_shapes=[pltpu.VMEM((N_HEADS_G, t_, KV_W), BF16),
                        pltpu.VMEM((t_ + 2 * WINDOW, KV_W), BF16),
                        pltpu.VMEM((nb + 2, KV_W, Q_BLOCK), BF16)],
        compiler_params=_params(),
        name="wattn",
    )(sink, h3, w, bias)


def _tile_scan(a, b, rev):
    sub = lax.broadcasted_iota(jnp.int32, a.shape, 0)
    for s in (1, 2, 4):
        if rev:
            a_sh = pltpu.roll(a, 8 - s, axis=0)
            b_sh = pltpu.roll(b, 8 - s, axis=0)
            ok = sub < 8 - s
        else:
            a_sh = pltpu.roll(a, s, axis=0)
            b_sh = pltpu.roll(b, s, axis=0)
            ok = sub >= s
        b = jnp.where(ok, a * b_sh + b, b)
        a = jnp.where(ok, a * a_sh, a)
    return a, b


def _lru_kernel(h_ref, w_ref, cw_ref, cb_ref, gw_ref, gb_ref, lam_ref, o_ref, xp_s, a_s, b_s, hs_s,
                *, t_, tr):
    zhalo = jnp.zeros((HALO, GROUP_W), F32)
    xp_s[0:HALO, :] = zhalo
    xp_s[HALO + t_:, :] = zhalo

    tp = min(ROW_TILE, t_)

    def proj(i, carry):
        r0 = pl.multiple_of(i * tp, tp)
        xp_s[pl.ds(pl.multiple_of(HALO + r0, HALO), tp), :] = _dot(h_ref[0, pl.ds(r0, tp), :], w_ref[:, 0:GROUP_W])
        return carry

    lax.fori_loop(0, t_ // tp, proj, 0)

    lam = lam_ref[...]
    sp = jnp.maximum(-lam, 0.0) + jnp.log(1.0 + jnp.exp(-jnp.abs(lam)))
    nsp = -LRU_C * sp

    def gates(i, carry):
        r0 = pl.multiple_of(i * tr, tr)
        rows = pl.ds(r0, tr)
        win = xp_s[pl.ds(r0, tr + 2 * HALO), :]
        xc = cb_ref[...]
        nw = tr + 2 * HALO
        for j in range(CONV_W):
            off = j - CONV_LEFT
            tap = win if off == 0 else pltpu.roll(win, (-off) % nw, axis=0)
            xc = xc + cw_ref[j:j + 1, :] * tap[HALO:HALO + tr, :]
        g = _dot(xc.astype(BF16), gw_ref[...]) + gb_ref[...]
        for d in range(2):
            r = _sigmoid(g[:, (2 * d) * GROUP_W:(2 * d + 1) * GROUP_W])
            ig = _sigmoid(g[:, (2 * d + 1) * GROUP_W:(2 * d + 2) * GROUP_W])
            a = jnp.exp(r * nsp[d:d + 1, :])
            a_s[d, rows, :] = a
            b_s[d, rows, :] = jnp.sqrt(1.0 - a * a) * (ig * xc)
        return carry

    lax.fori_loop(0, t_ // tr, gates, 0)

    nt = t_ // 8

    def tile(i, carry):
        cf, cr = carry
        rf = pl.multiple_of(i * 8, 8)
        rr = pl.multiple_of((nt - 1 - i) * 8, 8)
        af, bf = _tile_scan(a_s[0, pl.ds(rf, 8), :], b_s[0, pl.ds(rf, 8), :], False)
        ar, br = _tile_scan(a_s[1, pl.ds(rr, 8), :], b_s[1, pl.ds(rr, 8), :], True)
        hf = bf + af * cf
        hr = br + ar * cr
        hs_s[0, pl.ds(rf, 8), :] = hf
        hs_s[1, pl.ds(rr, 8), :] = hr
        return hf[7:8, :], hr[0:1, :]

    z = jnp.zeros((1, GROUP_W), F32)
    lax.fori_loop(0, nt, tile, (z, z), unroll=LRU_UNROLL if nt % LRU_UNROLL == 0 else 1)

    def epilogue(i, carry):
        r0 = pl.multiple_of(i * tr, tr)
        rows = pl.ds(r0, tr)
        gate = _dot(h_ref[0, rows, :], w_ref[:, GROUP_W:])
        o_ref[0, rows, :] = ((hs_s[0, rows, :] + hs_s[1, rows, :]) * _silu(gate)).astype(BF16)
        return carry

    lax.fori_loop(0, t_ // tr, epilogue, 0)


def _lru(h3, w, cw, cb, gw, gb, lam):
    b_, t_, _ = h3.shape
    return pl.pallas_call(
        functools.partial(_lru_kernel, t_=t_, tr=min(256, t_)),
        grid=(b_,),
        in_specs=[pl.BlockSpec((1, t_, D_MODEL), lambda b: (b, 0, 0)), _full(w), _full(cw), _full(cb),
                  _full(gw), _full(gb), _full(lam)],
        out_specs=pl.BlockSpec((1, t_, GROUP_W), lambda b: (b, 0, 0)),
        out_shape=jax.ShapeDtypeStruct((b_, t_, GROUP_W), BF16),
        scratch_shapes=[pltpu.VMEM((t_ + 2 * HALO, GROUP_W), F32),
                        pltpu.VMEM((2, t_, GROUP_W), F32), pltpu.VMEM((2, t_, GROUP_W), F32),
                        pltpu.VMEM((2, t_, GROUP_W), F32)],
        compiler_params=_params(),
        name="lru",
    )(h3, w, cw, cb, gw, gb, lam)


def _rwkv_tables():
    bd = _head_blockdiag()
    t = np.arange(CHUNK)
    tri = np.stack([(t[:, None] >= t[None, :]), (t[:, None] <= t[None, :])]).astype(np.float32)
    strict = np.stack([(t[:, None] > t[None, :]), (t[:, None] < t[None, :])]).astype(np.float32)
    incl = tri.copy()
    eye = np.eye(CHUNK, dtype=np.float32)
    wide = lambda m: np.tile(m, (1,) * (m.ndim - 1) + (N_HEADS_G,))
    return dict(bd=bd, bdb=bd, tri2=np.concatenate([tri, tri], axis=-1), strict=wide(strict), incl=wide(incl),
                eye=wide(eye))


def _rwkv_kernel(h_ref, w_ref, sh_ref, w0_ref, wup_ref, a0_ref, aup_ref, kk_ref, ka_ref, rk_ref,
                 lng_ref, lnb_ref, bd_ref, bdb_ref, tri_ref, strict_ref, incl_ref, eye_ref, o_ref,
                 zs_s, r_s, v_s, lg_s, pt_s, qd_s, kd_s, y_s, st_s, *, t_, tr):
    bd = bd_ref[...]
    bdb = bdb_ref[...]
    g = GROUP_W
    sw = RWKV_SHIFT_W
    cc = CHUNK
    bf = lambda x: x.astype(BF16)

    zhalo = jnp.zeros((HALO, sw), F32)
    zs_s[0:HALO, :] = zhalo
    zs_s[HALO + t_:, :] = zhalo

    tp = min(ROW_TILE, t_)

    def proj(i, carry):
        r0 = pl.multiple_of(i * tp, tp)
        zs_s[pl.ds(pl.multiple_of(HALO + r0, HALO), tp), :] = _dot(h_ref[0, pl.ds(r0, tp), :], w_ref[:, 0:sw])
        return carry

    lax.fori_loop(0, t_ // tp, proj, 0)

    ka1 = 1.0 - ka_ref[...]

    def prep(i, carry):
        r0 = pl.multiple_of(i * tr, tr)
        rows = pl.ds(r0, tr)

        def mixed(c0, c1):
            win = zs_s[pl.ds(r0, tr + 2 * HALO), c0:c1]
            nw = tr + 2 * HALO
            x = win[HALO:HALO + tr, :]
            prev = pltpu.roll(win, 1, axis=0)[HALO:HALO + tr, :]
            nxt = pltpu.roll(win, nw - 1, axis=0)[HALO:HALO + tr, :]
            return x + sh_ref[0:1, c0:c1] * (prev - x) + sh_ref[1:2, c0:c1] * (nxt - x)

        r_s[rows, :] = mixed(0, g)
        v_s[rows, :] = mixed(2 * g, 3 * g)
        k = mixed(g, 2 * g)
        lo = mixed(3 * g, sw)
        kk = k * kk_ref[...]
        ssq = _dot(bf(kk * kk), bdb)
        kk = kk * lax.rsqrt(jnp.maximum(ssq, 1e-24))
        dw = _dot(bf(jnp.tanh(lo)), wup_ref[...])
        da = _dot(bf(lo), aup_ref[...])
        for d in range(2):
            lw = -EXP_M05 * _sigmoid(w0_ref[d:d + 1, :] + dw[:, d * g:(d + 1) * g])
            a = _sigmoid(a0_ref[d:d + 1, :] + da[:, d * g:(d + 1) * g])
            kd_s[d, rows, :] = k * (ka1 + a * ka_ref[...])
            qd_s[d, rows, :] = kk * a
            lw_hi = bf(lw)
            lw_lo = bf(lw - lw_hi.astype(F32))
            lgs = [_dot(tri_ref[d], jnp.concatenate([lw_hi[c0:c0 + cc, :], lw_lo[c0:c0 + cc, :]], axis=0))
                   for c0 in range(0, tr, cc)]
            lg = jnp.concatenate(lgs, axis=0) if len(lgs) > 1 else lgs[0]
            lg_s[d, rows, :] = lg
            pt_s[d, rows, :] = kk * jnp.exp(lg - lw)
        return carry

    lax.fori_loop(0, t_ // tr, prep, 0)
    st_s[...] = jnp.zeros((2, g, g), F32)

    nc = t_ // cc
    ug = min(RWKV_GROUP, nc)
    rep = lambda x: jnp.concatenate([bf(x)] * N_HEADS_G, axis=0) * bdb
    stack = lambda a, b: bf(jnp.concatenate([a, b], axis=0))
    eye = eye_ref[...]

    def group(i, carry):
        ch = []
        for j in range(ug):
            for d in range(2):
                c = i * ug + j
                c = c if d == 0 else nc - 1 - c
                rows = pl.ds(pl.multiple_of(c * cc, cc), cc)
                lg = lg_s[d, rows, :]
                tot = lg[cc - 1:cc, :] if d == 0 else lg[0:1, :]
                g_inv = jnp.exp(-lg)
                g_end = jnp.exp(tot - lg)
                qdc = qd_s[d, rows, :]
                kdc = kd_s[d, rows, :]
                ch.append(dict(d=d, rows=rows, gtot=jnp.exp(tot), pt=pt_s[d, rows, :],
                               rt=r_s[rows, :] * jnp.exp(lg), qt=qdc * g_inv, kt=kdc * g_inv,
                               qe=qdc * g_end, ke=kdc * g_end, vc=v_s[rows, :]))
        for s in ch:
            pr = stack(s["pt"], s["rt"])
            a_q = _dot_nt(pr, rep(s["qt"]))
            a_k = _dot_nt(pr, rep(s["kt"]))
            strict = strict_ref[s["d"]]
            incl = incl_ref[s["d"]]
            s["a_rq"] = bf(a_q[cc:, :] * incl)
            s["a_k"] = stack(a_k[:cc, :] * strict, a_k[cc:, :] * incl)
            s["x"] = -(a_q[:cc, :] * strict)
            s["tm"] = eye + s["x"]
        for s in ch:
            s["x"] = _dot(bf(s["x"]), rep(s["x"]))
        for k in range(5):
            for s in ch:
                if k < 4:
                    both = _dot(stack(s["x"], s["tm"]), rep(s["x"]))
                    s["x"] = both[:cc, :]
                    s["tm"] = s["tm"] + both[cc:, :]
                else:
                    s["tm"] = s["tm"] + _dot(bf(s["tm"]), rep(s["x"]))
        for s in ch:
            s["tmb"] = bf(s["tm"])
            s["wm"] = _dot(s["tmb"], rep(s["pt"]))
            s["av"] = _dot(s["a_k"], rep(s["vc"]))
        for s in ch:
            s["u0"] = _dot(s["tmb"], rep(s["av"][:cc, :]))
            s["rm"] = bf(s["rt"] - _dot(s["a_rq"], rep(s["wm"])))
            s["mm"] = bf(_dot_tn(s["wm"], s["qe"]) * bd)
        for s in ch:
            s["y0"] = s["av"][cc:, :] - _dot(s["a_rq"], rep(s["u0"]))
            s["nn"] = _dot_tn(jnp.concatenate([s["vc"], -s["u0"]], axis=0),
                              jnp.concatenate([s["ke"], s["qe"]], axis=0)) * bd
        sts = [st_s[0], st_s[1]]
        for j in range(ug):
            for d in range(2):
                s = ch[2 * j + d]
                sb = bf(sts[d])
                upd = _dot(sb, s["mm"])
                y_s[d, s["rows"], :] = _dot_nt(s["rm"], sb) + s["y0"]
                sts[d] = sts[d] * s["gtot"] - upd + s["nn"]
        st_s[0] = sts[0]
        st_s[1] = sts[1]
        return carry

    lax.fori_loop(0, nc // ug, group, 0)

    def epilogue(i, carry):
        r0 = pl.multiple_of(i * tp, tp)
        rows = pl.ds(r0, tp)
        y = y_s[0, rows, :] + y_s[1, rows, :]
        mu = _dot(bf(y), bdb) * (1.0 / HEAD_DIM)
        yc = y - mu
        var = _dot(bf(yc * yc), bdb) * (1.0 / HEAD_DIM)
        ksum = kd_s[0, rows, :] + kd_s[1, rows, :]
        bonus = _dot(bf(r_s[rows, :] * ksum * rk_ref[...]), bdb) * v_s[rows, :]
        yn = yc * lax.rsqrt(var + GN_EPS) * lng_ref[...] + lnb_ref[...] + bonus
        gate = _dot(h_ref[0, rows, :], w_ref[:, sw:])
        o_ref[0, rows, :] = (yn * _silu(gate)).astype(BF16)
        return carry

    lax.fori_loop(0, t_ // tp, epilogue, 0)


def _rwkv(h3, w, p, tabs):
    b_, t_, _ = h3.shape
    args = [h3, w, p["shift"], p["w0"], p["wup"], p["a0"], p["aup"], p["kk"], p["ka"], p["rk"], p["lng"],
            p["lnb"], tabs["bd"], tabs["bdb"], tabs["tri2"], tabs["strict"], tabs["incl"], tabs["eye"]]
    big = lambda: pltpu.VMEM((t_, GROUP_W), F32)
    big2 = lambda: pltpu.VMEM((2, t_, GROUP_W), F32)
    return pl.pallas_call(
        functools.partial(_rwkv_kernel, t_=t_, tr=min(128, t_)),
        grid=(b_,),
        in_specs=[pl.BlockSpec((1, t_, D_MODEL), lambda b: (b, 0, 0))] + [_full(a) for a in args[1:]],
        out_specs=pl.BlockSpec((1, t_, GROUP_W), lambda b: (b, 0, 0)),
        out_shape=jax.ShapeDtypeStruct((b_, t_, GROUP_W), BF16),
        scratch_shapes=[pltpu.VMEM((t_ + 2 * HALO, RWKV_SHIFT_W), F32),
                        big(), big(), big2(), big2(), big2(), big2(), big2(),
                        pltpu.VMEM((2, GROUP_W, GROUP_W), F32)],
        compiler_params=_params(),
        name="rwkv",
    )(*args)


def _prep_layer(l, P):
    w_in = P["w_in"][l].astype(BF16)
    c0, c1, c2 = A_W, A_W + B_W, A_W + B_W + C_W
    pad = jnp.zeros((2, RWKV_DECAY_RANK, GROUP_W), F32)
    wup = jnp.concatenate([P["rwkv_w_up"][l], pad], axis=1)
    aup = jnp.concatenate([pad, P["rwkv_a_up"][l]], axis=1)
    cat2 = lambda m: jnp.concatenate([m[0], m[1]], axis=1).astype(BF16)
    gw = P["lru_gate_w"][l]
    blocks = []
    for d in range(2):
        for kk in range(2):
            blocks.append(jax.scipy.linalg.block_diag(*[gw[d, kk, n] for n in range(LRU_BLOCKS)]))
    row = lambda a: a.reshape(1, -1)
    return dict(
        wA=w_in[:, :c0], wB=w_in[:, c0:c1], wC=w_in[:, c1:c2], wD=w_in[:, c2:],
        w_out=P["w_out"][l].astype(BF16),
        rwkv=dict(shift=P["rwkv_shift"][l], w0=P["rwkv_w0"][l], wup=cat2(wup), a0=P["rwkv_a0"][l],
                  aup=cat2(aup), kk=row(P["rwkv_k_k"][l]), ka=row(P["rwkv_k_a"][l]),
                  rk=row(P["rwkv_r_k"][l]), lng=row(P["rwkv_ln_g"][l]), lnb=row(P["rwkv_ln_b"][l])),
        qw=jnp.tile(P["attn_q_norm"][l], 2).reshape(1, -1), kw=jnp.tile(P["attn_k_norm"][l], 2).reshape(1, -1),
        cw=P["lru_conv_w"][l], cb=row(P["lru_conv_b"][l]),
        gw=jnp.concatenate(blocks, axis=1).astype(BF16), gb=P["lru_gate_b"][l].reshape(1, -1),
        lam=P["lru_lambda"][l], sink=P["swa_sink"][l],
    )


def _tables(t_):
    cos, sin = _rope_tables(t_)
    qi = np.arange(Q_BLOCK)[:, None]
    ki = np.arange(3 * Q_BLOCK)[None, :]
    dist = np.abs(ki - WINDOW - qi).astype(np.float32)
    slopes = np.exp2(-8.0 * np.arange(1, N_HEADS_G + 1, dtype=np.float32) / N_HEADS_G).astype(np.float32)
    alibi = np.where(dist[None] <= WINDOW, -slopes[:, None, None] * dist[None], NEG).astype(np.float32)
    first = np.where(ki[None] < Q_BLOCK, NEG, alibi).astype(np.float32)
    last = np.where(ki[None] >= 2 * Q_BLOCK, NEG, alibi).astype(np.float32)
    wbias = np.ascontiguousarray(np.stack([alibi, first, last]).transpose(0, 1, 3, 2))
    tabs = dict(cos=cos, sin=sin, seg128=_head_blockdiag()[:128, :128], wbias=wbias)
    tabs.update(_rwkv_tables())
    out = {k: jnp.asarray(v) for k, v in tabs.items()}
    for k in ("bdb", "tri2", "seg128"):
        out[k] = out[k].astype(BF16)
    return out


def _trunk(x, P, layers):
    b_, t_, _ = x.shape
    tabs = _tables(t_)
    x2 = x.reshape(b_ * t_, D_MODEL)
    h2 = _norm(x2, P["norm_g"][0].reshape(1, -1))
    for l in range(DEPTH):
        lp = layers[l]
        h3 = h2.reshape(b_, t_, D_MODEL)
        oa = _rwkv(h3, lp["wA"], lp["rwkv"], tabs)
        ob = _gattn(h3, lp["wB"], lp["qw"], lp["kw"], tabs)
        oc = _lru(h3, lp["wC"], lp["cw"], lp["cb"], lp["gw"], lp["gb"], lp["lam"])
        od = _wattn(h3, lp["wD"], lp["sink"], tabs)
        flat = lambda o: o.reshape(b_ * t_, GROUP_W)
        final = l == DEPTH - 1
        g_next = (P["final_g"] if final else P["norm_g"][l + 1]).reshape(1, -1)
        outs = _out_proj(flat(oa), flat(ob), flat(oc), flat(od), lp["w_out"], x2, g_next, final)
        if final:
            x2 = outs[0]
        else:
            x2, h2 = outs
    return x2.reshape(b_, t_, D_MODEL)


def kernel(x_prompt, x_sample, norm_g, w_in, w_out, rwkv_shift, rwkv_w0, rwkv_w_up, rwkv_a0, rwkv_a_up,
           rwkv_k_k, rwkv_k_a, rwkv_r_k, rwkv_ln_g, rwkv_ln_b, attn_q_norm, attn_k_norm, lru_conv_w,
           lru_conv_b, lru_gate_w, lru_gate_b, lru_lambda, swa_sink, final_g):
    P = dict(norm_g=norm_g, w_in=w_in, w_out=w_out, rwkv_shift=rwkv_shift, rwkv_w0=rwkv_w0,
             rwkv_w_up=rwkv_w_up, rwkv_a0=rwkv_a0, rwkv_a_up=rwkv_a_up, rwkv_k_k=rwkv_k_k,
             rwkv_k_a=rwkv_k_a, rwkv_r_k=rwkv_r_k, rwkv_ln_g=rwkv_ln_g, rwkv_ln_b=rwkv_ln_b,
             attn_q_norm=attn_q_norm, attn_k_norm=attn_k_norm, lru_conv_w=lru_conv_w,
             lru_conv_b=lru_conv_b, lru_gate_w=lru_gate_w, lru_gate_b=lru_gate_b,
             lru_lambda=lru_lambda, swa_sink=swa_sink, final_g=final_g)
    layers = [_prep_layer(l, P) for l in range(DEPTH)]
    return _trunk(x_prompt, P, layers), _trunk(x_sample, P, layers)
```

```python
import functools
import math

import jax
import jax.numpy as jnp
import numpy as np
from jax import lax
from jax.experimental import pallas as pl
from jax.experimental.pallas import tpu as pltpu

D_MODEL = 1024
DEPTH = 4
GRID_W = 64
HEAD_DIM = 64
GROUP_W = 256
N_HEADS_G = 4
N_KV = 2
REP = 2
KV_W = 128
RWKV_DECAY_RANK = 64
RWKV_SHIFT_W = 896
LRU_C = 8.0
LRU_BLOCKS = 4
LRU_BLK = 64
CONV_W = 4
CONV_LEFT = 2
Q_BLOCK = 128
WINDOW = 128
ROPE_THETA = 10000.0
NORM_EPS = 1e-6
GN_EPS = 64e-5
NEG = -1e30
A_W = 1152
B_W = 768
C_W = 512
D_W = 768

CHUNK = 64
RWKV_GROUP = 4
LRU_UNROLL = 4
GATTN_BLOCKS = 4
WATTN_BLOCKS = 8
ROW_TILE = 512
OUT_TILE = 1024
VT_ROWS = 80
HALO = 8
V7X_VMEM_LIMIT = 56 * 1024 * 1024
F32 = jnp.float32
BF16 = jnp.bfloat16
HI = lax.Precision.HIGHEST
EXP_M05 = math.exp(-0.5)
LOG2E = math.log2(math.e)


def _dot(a, b, prec=None):
    return jnp.dot(a, b, preferred_element_type=F32, precision=prec)


def _dot_nt(a, b, prec=None):
    return lax.dot_general(a, b, (((1,), (1,)), ((), ())), preferred_element_type=F32, precision=prec)


def _dot_tn(a, b, prec=None):
    return lax.dot_general(a, b, (((0,), (0,)), ((), ())), preferred_element_type=F32, precision=prec)


def _sigmoid(x):
    return 1.0 / (1.0 + jnp.exp(-x))


def _silu(x):
    return x * _sigmoid(x)


def _rms(x, g):
    return x * lax.rsqrt(jnp.mean(x * x, axis=-1, keepdims=True) + NORM_EPS) * g


def _params(n_axes=1):
    return pltpu.CompilerParams(dimension_semantics=("arbitrary",) * n_axes,
                                vmem_limit_bytes=V7X_VMEM_LIMIT)


def _full(a):
    nd = a.ndim
    return pl.BlockSpec(a.shape, lambda *_: (0,) * nd)


def _norm_kernel(x_ref, g_ref, h_ref):
    h_ref[...] = _rms(x_ref[...], g_ref[...]).astype(BF16)


def _norm(x2, g):
    m = x2.shape[0]
    tm = min(ROW_TILE, m)
    return pl.pallas_call(
        _norm_kernel,
        grid=(m // tm,),
        in_specs=[pl.BlockSpec((tm, D_MODEL), lambda i: (i, 0)), _full(g)],
        out_specs=pl.BlockSpec((tm, D_MODEL), lambda i: (i, 0)),
        out_shape=jax.ShapeDtypeStruct((m, D_MODEL), BF16),
        compiler_params=_params(),
        name="norm",
    )(x2, g)


def _out_kernel(oa_ref, ob_ref, oc_ref, od_ref, w_ref, x_ref, g_ref, *out_refs, final):
    acc = _dot(oa_ref[...], w_ref[0:GROUP_W, :])
    acc += _dot(ob_ref[...], w_ref[GROUP_W:2 * GROUP_W, :])
    acc += _dot(oc_ref[...], w_ref[2 * GROUP_W:3 * GROUP_W, :])
    acc += _dot(od_ref[...], w_ref[3 * GROUP_W:4 * GROUP_W, :])
    xn = x_ref[...] + acc
    if final:
        out_refs[0][...] = _rms(xn, g_ref[...])
    else:
        out_refs[0][...] = xn
        out_refs[1][...] = _rms(xn, g_ref[...]).astype(BF16)


def _out_proj(oa, ob, oc, od, w, x2, g, final):
    m = x2.shape[0]
    tm = min(OUT_TILE, m)
    ospec = pl.BlockSpec((tm, GROUP_W), lambda i: (i, 0))
    xspec = pl.BlockSpec((tm, D_MODEL), lambda i: (i, 0))
    if final:
        out_shape = (jax.ShapeDtypeStruct((m, D_MODEL), F32),)
        out_specs = (xspec,)
    else:
        out_shape = (jax.ShapeDtypeStruct((m, D_MODEL), F32), jax.ShapeDtypeStruct((m, D_MODEL), BF16))
        out_specs = (xspec, xspec)
    return pl.pallas_call(
        functools.partial(_out_kernel, final=final),
        grid=(m // tm,),
        in_specs=[ospec, ospec, ospec, ospec, _full(w), xspec, _full(g)],
        out_specs=out_specs,
        out_shape=out_shape,
        compiler_params=_params(),
        name="out_proj",
    )(oa, ob, oc, od, w, x2, g)


def _head_blockdiag():
    i = np.arange(GROUP_W)
    return (i[:, None] // HEAD_DIM == i[None, :] // HEAD_DIM).astype(np.float32)


def _rope_tables(t_):
    rows = t_ // GRID_W
    row = np.repeat(np.arange(rows), GRID_W).astype(np.float32)
    col = np.tile(np.arange(GRID_W), rows).astype(np.float32)
    half = HEAD_DIM // 2
    inv = (ROPE_THETA ** (-np.arange(0, half, 2, dtype=np.float32) / half)).astype(np.float32)
    ang_r = row[:, None] * inv
    ang_c = col[:, None] * inv
    cos = np.concatenate([np.cos(ang_r), np.cos(ang_r), np.cos(ang_c), np.cos(ang_c)], -1)
    sin = np.concatenate([-np.sin(ang_r), np.sin(ang_r), -np.sin(ang_c), np.sin(ang_c)], -1)
    return (np.tile(cos, (1, 2)).astype(np.float32), np.tile(sin, (1, 2)).astype(np.float32))


def _swap16(x):
    n = x.shape[-1]
    up = pltpu.roll(x, n - 16, axis=1)
    dn = pltpu.roll(x, 16, axis=1)
    lane = lax.broadcasted_iota(jnp.int32, x.shape, 1)
    return jnp.where((lane % 32) < 16, up, dn)


def _place_heads(half0, half1):
    lo = lax.broadcasted_iota(jnp.int32, half0.shape, 1) < HEAD_DIM
    z = jnp.zeros_like(half0)
    return [jnp.where(lo, half0, z), jnp.where(lo, pltpu.roll(half0, HEAD_DIM, axis=1), z),
            jnp.where(lo, z, pltpu.roll(half1, HEAD_DIM, axis=1)), jnp.where(lo, z, half1)]


def _gattn_kernel(h_ref, w_ref, qw_ref, kw_ref, cos_ref, sin_ref, seg_ref, o_ref,
                  qh_s, k_s, vt_s, sg_s, *, t_, tq, tr):
    seg = seg_ref[...]

    def prologue(i, carry):
        r0 = pl.multiple_of(i * tr, tr)
        rows = pl.ds(r0, tr)
        h = h_ref[0, rows, :]
        cos = cos_ref[rows, :]
        sin = sin_ref[rows, :]

        def normrope(z, wgt):
            ms = _dot((z * z).astype(BF16), seg) * (1.0 / HEAD_DIM)
            z = z * lax.rsqrt(ms + NORM_EPS) * wgt
            return z * cos + _swap16(z) * sin

        z = _dot(h, w_ref[...])
        sg_s[rows, :] = _silu(z[:, GROUP_W + 2 * KV_W:])
        halves = [normrope(z[:, c * 128:(c + 1) * 128], qw_ref[...]) * (HEAD_DIM ** -0.5 * LOG2E)
                  for c in range(2)]
        for hd, qm in enumerate(_place_heads(*halves)):
            qh_s[hd, rows, :] = qm.astype(BF16)
        k_s[rows, :] = normrope(z[:, GROUP_W:GROUP_W + KV_W], kw_ref[...]).astype(BF16)
        vt = z[:, GROUP_W + KV_W:GROUP_W + 2 * KV_W].T.astype(BF16)
        ones = jnp.ones((VT_ROWS - HEAD_DIM, tr), BF16)
        for kv in range(N_KV):
            vt_s[i, kv * VT_ROWS:kv * VT_ROWS + HEAD_DIM, :] = vt[kv * HEAD_DIM:(kv + 1) * HEAD_DIM, :]
            vt_s[i, kv * VT_ROWS + HEAD_DIM:(kv + 1) * VT_ROWS, :] = ones
        return carry

    lax.fori_loop(0, t_ // tr, prologue, 0)
    nvc = t_ // tr

    nqb = t_ // tq
    gb = min(GATTN_BLOCKS, nqb)

    def qblocks(i, carry):
        rows = [pl.ds(pl.multiple_of((i * gb + j) * tq, tq), tq) for j in range(gb)]
        units = [(j, pair) for j in range(gb) for pair in ((0, 1), (2, 3))]
        scores = lambda u: [_dot_nt(k_s[...], qh_s[hd, rows[u[0]], :]) for hd in u[1]]
        outs = [[] for _ in range(gb)]
        s_next = scores(units[0])
        for ui, (j, pair) in enumerate(units):
            ss = s_next
            if ui + 1 < len(units):
                s_next = scores(units[ui + 1])
            st = [dict(m=jnp.full((1, tq), NEG, F32), ot=jnp.zeros((VT_ROWS, tq), F32)) for _ in pair]
            for c in range(nvc):
                for k, hd in enumerate(pair):
                    g0 = (hd // REP) * VT_ROWS
                    a = st[k]
                    sc = ss[k][c * tr:(c + 1) * tr, :]
                    mc = jnp.maximum(a["m"], jnp.max(sc, axis=0, keepdims=True))
                    alpha = jnp.exp2(a["m"] - mc)
                    p = jnp.exp2(sc - mc)
                    a["ot"] = a["ot"] * alpha + _dot(vt_s[c, g0:g0 + VT_ROWS, :], p.astype(BF16))
                    a["m"] = mc
            outs[j] += [a["ot"][:HEAD_DIM, :] * (1.0 / a["ot"][HEAD_DIM:HEAD_DIM + 1, :]) for a in st]
        for j in range(gb):
            o = jnp.concatenate(outs[j], axis=0).T
            o_ref[0, rows[j], :] = (o * sg_s[rows[j], :]).astype(BF16)
        return carry

    lax.fori_loop(0, nqb // gb, qblocks, 0)


def _gattn(h3, w, qw, kw, tabs):
    b_, t_, _ = h3.shape
    tq = min(256, t_)
    tr = min(ROW_TILE, t_)
    cos, sin, seg = tabs["cos"], tabs["sin"], tabs["seg128"]
    return pl.pallas_call(
        functools.partial(_gattn_kernel, t_=t_, tq=tq, tr=tr),
        grid=(b_,),
        in_specs=[pl.BlockSpec((1, t_, D_MODEL), lambda b: (b, 0, 0)), _full(w), _full(qw), _full(kw),
                  _full(cos), _full(sin), _full(seg)],
        out_specs=pl.BlockSpec((1, t_, GROUP_W), lambda b: (b, 0, 0)),
        out_shape=jax.ShapeDtypeStruct((b_, t_, GROUP_W), BF16),
        scratch_shapes=[pltpu.VMEM((N_HEADS_G, t_, KV_W), BF16),
                        pltpu.VMEM((t_, KV_W), BF16),
                        pltpu.VMEM((t_ // tr, N_KV * VT_ROWS, tr), BF16),
                        pltpu.VMEM((t_, GROUP_W), F32)],
        compiler_params=_params(),
        name="gattn",
    )(h3, w, qw, kw, cos, sin, seg)


def _wattn_kernel(sink_ref, h_ref, w_ref, bias_ref, o_ref, qh_s, kp_s, vt_s, sg_s, *, t_, tr):
    nb = t_ // Q_BLOCK
    zblk = jnp.zeros((Q_BLOCK, KV_W), BF16)
    kp_s[0:WINDOW, :] = zblk
    kp_s[WINDOW + t_:, :] = zblk
    vt_s[0] = zblk
    vt_s[nb + 1] = zblk
    bpt = tr // Q_BLOCK

    def prologue(i, carry):
        r0 = pl.multiple_of(i * tr, tr)
        rows = pl.ds(r0, tr)
        h = h_ref[0, rows, :]
        z = _dot(h, w_ref[...])
        sg_s[rows, :] = _silu(z[:, GROUP_W + 2 * KV_W:])
        zq = z[:, 0:GROUP_W] * (HEAD_DIM ** -0.5)
        for hd, qm in enumerate(_place_heads(zq[:, 0:KV_W], zq[:, KV_W:])):
            qh_s[hd, rows, :] = qm.astype(BF16)
        kp_s[pl.ds(pl.multiple_of(WINDOW + r0, WINDOW), tr), :] = z[:, GROUP_W:GROUP_W + KV_W].astype(BF16)
        zv = z[:, GROUP_W + KV_W:GROUP_W + 2 * KV_W]
        for jb in range(bpt):
            vt_s[1 + i * bpt + jb] = zv[jb * Q_BLOCK:(jb + 1) * Q_BLOCK, :].T.astype(BF16)
        return carry

    lax.fori_loop(0, t_ // tr, prologue, 0)

    kw = 3 * Q_BLOCK
    npb = min(WATTN_BLOCKS, nb)

    def qblocks(i, carry):
        items = []
        for j in range(npb):
            blk = i * npb + j
            r0 = pl.multiple_of(blk * Q_BLOCK, Q_BLOCK)
            var = jnp.where(blk == 0, 1, jnp.where(blk == nb - 1, 2, 0))
            kwin = kp_s[pl.ds(r0, kw), :]
            for hd in range(N_HEADS_G):
                items.append((j, hd, blk, var, _dot_nt(kwin, qh_s[hd, pl.ds(r0, Q_BLOCK), :])))
        outs = [[None] * N_HEADS_G for _ in range(npb)]
        for j, hd, blk, var, s in items:
            s = s + bias_ref[var, hd]
            sk = sink_ref[hd]
            m = jnp.maximum(jnp.max(s, axis=0, keepdims=True), sk)
            p = jnp.exp(s - m)
            den = jnp.sum(p, axis=0, keepdims=True) + jnp.exp(sk - m)
            pb = p.astype(BF16)
            g0 = (hd // REP) * HEAD_DIM
            ot = _dot(vt_s[blk, g0:g0 + HEAD_DIM, :], pb[0:Q_BLOCK, :])
            for jj in range(1, 3):
                ot += _dot(vt_s[blk + jj, g0:g0 + HEAD_DIM, :], pb[jj * Q_BLOCK:(jj + 1) * Q_BLOCK, :])
            outs[j][hd] = ot * (1.0 / den)
        for j in range(npb):
            r0 = pl.multiple_of((i * npb + j) * Q_BLOCK, Q_BLOCK)
            o = jnp.concatenate(outs[j], axis=0).T
            o_ref[0, pl.ds(r0, Q_BLOCK), :] = (o * sg_s[pl.ds(r0, Q_BLOCK), :]).astype(BF16)
        return carry

    lax.fori_loop(0, nb // npb, qblocks, 0)


def _wattn(h3, w, sink, tabs):
    b_, t_, _ = h3.shape
    bias = tabs["wbias"]
    nb = t_ // Q_BLOCK
    assert nb >= 2
    return pl.pallas_call(
        functools.partial(_wattn_kernel, t_=t_, tr=min(ROW_TILE, t_)),
        grid=(b_,),
        in_specs=[pl.BlockSpec(memory_space=pltpu.SMEM),
                  pl.BlockSpec((1, t_, D_MODEL), lambda b: (b, 0, 0)), _full(w), _full(bias)],
        out_specs=pl.BlockSpec((1, t_, GROUP_W), lambda b: (b, 0, 0)),
        out_shape=jax.ShapeDtypeStruct((b_, t_, GROUP_W), BF16),
        scratch_shapes=[pltpu.VMEM((N_HEADS_G, t_, KV_W), BF16),
                        pltpu.VMEM((t_ + 2 * WINDOW, KV_W), BF16),
                        pltpu.VMEM((nb + 2, KV_W, Q_BLOCK), BF16),
                        pltpu.VMEM((t_, GROUP_W), F32)],
        compiler_params=_params(),
        name="wattn",
    )(sink, h3, w, bias)


def _tile_scan(a, b, rev):
    sub = lax.broadcasted_iota(jnp.int32, a.shape, 0)
    for s in (1, 2, 4):
        if rev:
            a_sh = pltpu.roll(a, 8 - s, axis=0)
            b_sh = pltpu.roll(b, 8 - s, axis=0)
            ok = sub < 8 - s
        else:
            a_sh = pltpu.roll(a, s, axis=0)
            b_sh = pltpu.roll(b, s, axis=0)
            ok = sub >= s
        b = jnp.where(ok, a * b_sh + b, b)
        a = jnp.where(ok, a * a_sh, a)
    return a, b


def _lru_kernel(h_ref, w_ref, cw_ref, cb_ref, gw_ref, gb_ref, lam_ref, o_ref, xp_s, a_s, b_s, hs_s, sg_s,
                *, t_, tr):
    zhalo = jnp.zeros((HALO, GROUP_W), F32)
    xp_s[0:HALO, :] = zhalo
    xp_s[HALO + t_:, :] = zhalo

    tp = min(ROW_TILE, t_)

    def proj(i, carry):
        r0 = pl.multiple_of(i * tp, tp)
        z = _dot(h_ref[0, pl.ds(r0, tp), :], w_ref[...])
        xp_s[pl.ds(pl.multiple_of(HALO + r0, HALO), tp), :] = z[:, 0:GROUP_W]
        sg_s[pl.ds(r0, tp), :] = _silu(z[:, GROUP_W:])
        return carry

    lax.fori_loop(0, t_ // tp, proj, 0)

    lam = lam_ref[...]
    sp = jnp.maximum(-lam, 0.0) + jnp.log(1.0 + jnp.exp(-jnp.abs(lam)))
    nsp = -LRU_C * sp

    def gates(i, carry):
        r0 = pl.multiple_of(i * tr, tr)
        rows = pl.ds(r0, tr)
        win = xp_s[pl.ds(r0, tr + 2 * HALO), :]
        xc = cb_ref[...]
        nw = tr + 2 * HALO
        for j in range(CONV_W):
            off = j - CONV_LEFT
            tap = win if off == 0 else pltpu.roll(win, (-off) % nw, axis=0)
            xc = xc + cw_ref[j:j + 1, :] * tap[HALO:HALO + tr, :]
        g = _dot(xc.astype(BF16), gw_ref[...]) + gb_ref[...]
        for d in range(2):
            r = _sigmoid(g[:, (2 * d) * GROUP_W:(2 * d + 1) * GROUP_W])
            ig = _sigmoid(g[:, (2 * d + 1) * GROUP_W:(2 * d + 2) * GROUP_W])
            a = jnp.exp(r * nsp[d:d + 1, :])
            a_s[d, rows, :] = a
            b_s[d, rows, :] = jnp.sqrt(1.0 - a * a) * (ig * xc)
        return carry

    lax.fori_loop(0, t_ // tr, gates, 0)

    nt = t_ // 8

    def tile(i, carry):
        cf, cr = carry
        rf = pl.multiple_of(i * 8, 8)
        rr = pl.multiple_of((nt - 1 - i) * 8, 8)
        af, bf = _tile_scan(a_s[0, pl.ds(rf, 8), :], b_s[0, pl.ds(rf, 8), :], False)
        ar, br = _tile_scan(a_s[1, pl.ds(rr, 8), :], b_s[1, pl.ds(rr, 8), :], True)
        hf = bf + af * cf
        hr = br + ar * cr
        hs_s[0, pl.ds(rf, 8), :] = hf
        hs_s[1, pl.ds(rr, 8), :] = hr
        return hf[7:8, :], hr[0:1, :]

    z = jnp.zeros((1, GROUP_W), F32)
    lax.fori_loop(0, nt, tile, (z, z), unroll=LRU_UNROLL if nt % LRU_UNROLL == 0 else 1)

    def epilogue(i, carry):
        rows = pl.ds(pl.multiple_of(i * tp, tp), tp)
        o_ref[0, rows, :] = ((hs_s[0, rows, :] + hs_s[1, rows, :]) * sg_s[rows, :]).astype(BF16)
        return carry

    lax.fori_loop(0, t_ // tp, epilogue, 0)


def _lru(h3, w, cw, cb, gw, gb, lam):
    b_, t_, _ = h3.shape
    return pl.pallas_call(
        functools.partial(_lru_kernel, t_=t_, tr=min(256, t_)),
        grid=(b_,),
        in_specs=[pl.BlockSpec((1, t_, D_MODEL), lambda b: (b, 0, 0)), _full(w), _full(cw), _full(cb),
                  _full(gw), _full(gb), _full(lam)],
        out_specs=pl.BlockSpec((1, t_, GROUP_W), lambda b: (b, 0, 0)),
        out_shape=jax.ShapeDtypeStruct((b_, t_, GROUP_W), BF16),
        scratch_shapes=[pltpu.VMEM((t_ + 2 * HALO, GROUP_W), F32),
                        pltpu.VMEM((2, t_, GROUP_W), F32), pltpu.VMEM((2, t_, GROUP_W), F32),
                        pltpu.VMEM((2, t_, GROUP_W), F32), pltpu.VMEM((t_, GROUP_W), F32)],
        compiler_params=_params(),
        name="lru",
    )(h3, w, cw, cb, gw, gb, lam)


def _rwkv_tables():
    bd = _head_blockdiag()
    t = np.arange(CHUNK)
    tri = np.stack([(t[:, None] >= t[None, :]), (t[:, None] <= t[None, :])]).astype(np.float32)
    strict = np.stack([(t[:, None] > t[None, :]), (t[:, None] < t[None, :])]).astype(np.float32)
    incl = tri.copy()
    eye = np.eye(CHUNK, dtype=np.float32)
    wide = lambda m: np.tile(m, (1,) * (m.ndim - 1) + (N_HEADS_G,))
    return dict(bd=bd, bdb=bd, tri2=np.concatenate([tri, tri], axis=-1), strict=wide(strict), incl=wide(incl),
                eye=wide(eye))


def _rwkv_kernel(h_ref, w_ref, sh_ref, w0_ref, wup_ref, a0_ref, aup_ref, kk_ref, ka_ref, rk_ref,
                 lng_ref, lnb_ref, bd_ref, bdb_ref, tri_ref, strict_ref, incl_ref, eye_ref, o_ref,
                 zs_s, r_s, v_s, sg_s, lg_s, pt_s, qd_s, kd_s, y_s, st_s, *, t_, tr):
    bd = bd_ref[...]
    bdb = bdb_ref[...]
    g = GROUP_W
    sw = RWKV_SHIFT_W
    cc = CHUNK
    bf = lambda x: x.astype(BF16)

    zhalo = jnp.zeros((HALO, sw), F32)
    zs_s[0:HALO, :] = zhalo
    zs_s[HALO + t_:, :] = zhalo

    tp = min(ROW_TILE, t_)

    def proj(i, carry):
        r0 = pl.multiple_of(i * tp, tp)
        z = _dot(h_ref[0, pl.ds(r0, tp), :], w_ref[...])
        zs_s[pl.ds(pl.multiple_of(HALO + r0, HALO), tp), :] = z[:, 0:sw]
        sg_s[pl.ds(r0, tp), :] = _silu(z[:, sw:])
        return carry

    lax.fori_loop(0, t_ // tp, proj, 0)

    ka1 = 1.0 - ka_ref[...]

    def prep(i, carry):
        r0 = pl.multiple_of(i * tr, tr)
        rows = pl.ds(r0, tr)

        def mixed(c0, c1):
            win = zs_s[pl.ds(r0, tr + 2 * HALO), c0:c1]
            nw = tr + 2 * HALO
            x = win[HALO:HALO + tr, :]
            prev = pltpu.roll(win, 1, axis=0)[HALO:HALO + tr, :]
            nxt = pltpu.roll(win, nw - 1, axis=0)[HALO:HALO + tr, :]
            return x + sh_ref[0:1, c0:c1] * (prev - x) + sh_ref[1:2, c0:c1] * (nxt - x)

        r_s[rows, :] = mixed(0, g)
        v_s[rows, :] = mixed(2 * g, 3 * g)
        k = mixed(g, 2 * g)
        lo = mixed(3 * g, sw)
        kk = k * kk_ref[...]
        ssq = _dot(bf(kk * kk), bdb)
        kk = kk * lax.rsqrt(jnp.maximum(ssq, 1e-24))
        dw = _dot(bf(jnp.tanh(lo)), wup_ref[...])
        da = _dot(bf(lo), aup_ref[...])
        for d in range(2):
            lw = -EXP_M05 * _sigmoid(w0_ref[d:d + 1, :] + dw[:, d * g:(d + 1) * g])
            a = _sigmoid(a0_ref[d:d + 1, :] + da[:, d * g:(d + 1) * g])
            kd_s[d, rows, :] = k * (ka1 + a * ka_ref[...])
            qd_s[d, rows, :] = kk * a
            lw_hi = bf(lw)
            lw_lo = bf(lw - lw_hi.astype(F32))
            lgs = [_dot(tri_ref[d], jnp.concatenate([lw_hi[c0:c0 + cc, :], lw_lo[c0:c0 + cc, :]], axis=0))
                   for c0 in range(0, tr, cc)]
            lg = jnp.concatenate(lgs, axis=0) if len(lgs) > 1 else lgs[0]
            lg_s[d, rows, :] = lg
            pt_s[d, rows, :] = kk * jnp.exp(lg - lw)
        return carry

    lax.fori_loop(0, t_ // tr, prep, 0)
    st_s[...] = jnp.zeros((2, g, g), F32)

    nc = t_ // cc
    ug = min(RWKV_GROUP, nc)
    rep = lambda x: jnp.concatenate([bf(x)] * N_HEADS_G, axis=0) * bdb
    stack = lambda a, b: bf(jnp.concatenate([a, b], axis=0))
    eye = eye_ref[...]

    def group(i, carry):
        ch = []
        for j in range(ug):
            for d in range(2):
                c = i * ug + j
                c = c if d == 0 else nc - 1 - c
                rows = pl.ds(pl.multiple_of(c * cc, cc), cc)
                lg = lg_s[d, rows, :]
                tot = lg[cc - 1:cc, :] if d == 0 else lg[0:1, :]
                g_inv = jnp.exp(-lg)
                g_end = jnp.exp(tot - lg)
                qdc = qd_s[d, rows, :]
                kdc = kd_s[d, rows, :]
                ch.append(dict(d=d, rows=rows, gtot=jnp.exp(tot), pt=pt_s[d, rows, :],
                               rt=r_s[rows, :] * jnp.exp(lg), qt=qdc * g_inv, kt=kdc * g_inv,
                               qe=qdc * g_end, ke=kdc * g_end, vc=v_s[rows, :]))
        for s in ch:
            pr = stack(s["pt"], s["rt"])
            a_q = _dot_nt(pr, rep(s["qt"]))
            a_k = _dot_nt(pr, rep(s["kt"]))
            strict = strict_ref[s["d"]]
            incl = incl_ref[s["d"]]
            s["a_rq"] = bf(a_q[cc:, :] * incl)
            s["a_k"] = stack(a_k[:cc, :] * strict, a_k[cc:, :] * incl)
            s["x"] = -(a_q[:cc, :] * strict)
            s["tm"] = eye + s["x"]
        for s in ch:
            s["x"] = _dot(bf(s["x"]), rep(s["x"]))
        for k in range(5):
            for s in ch:
                if k < 4:
                    both = _dot(stack(s["x"], s["tm"]), rep(s["x"]))
                    s["x"] = both[:cc, :]
                    s["tm"] = s["tm"] + both[cc:, :]
                else:
                    s["tm"] = s["tm"] + _dot(bf(s["tm"]), rep(s["x"]))
        for s in ch:
            s["tmb"] = bf(s["tm"])
            s["wm"] = _dot(s["tmb"], rep(s["pt"]))
            s["av"] = _dot(s["a_k"], rep(s["vc"]))
        for s in ch:
            s["u0"] = _dot(s["tmb"], rep(s["av"][:cc, :]))
            s["rm"] = bf(s["rt"] - _dot(s["a_rq"], rep(s["wm"])))
            s["mm"] = bf(_dot_tn(s["wm"], s["qe"]) * bd)
        for s in ch:
            s["y0"] = s["av"][cc:, :] - _dot(s["a_rq"], rep(s["u0"]))
            s["nn"] = _dot_tn(jnp.concatenate([s["vc"], -s["u0"]], axis=0),
                              jnp.concatenate([s["ke"], s["qe"]], axis=0)) * bd
        sts = [st_s[0], st_s[1]]
        for j in range(ug):
            for d in range(2):
                s = ch[2 * j + d]
                sb = bf(sts[d])
                upd = _dot(sb, s["mm"])
                y_s[d, s["rows"], :] = _dot_nt(s["rm"], sb) + s["y0"]
                sts[d] = sts[d] * s["gtot"] - upd + s["nn"]
        st_s[0] = sts[0]
        st_s[1] = sts[1]
        return carry

    lax.fori_loop(0, nc // ug, group, 0)

    def epilogue(i, carry):
        r0 = pl.multiple_of(i * tp, tp)
        rows = pl.ds(r0, tp)
        y = y_s[0, rows, :] + y_s[1, rows, :]
        mu = _dot(bf(y), bdb) * (1.0 / HEAD_DIM)
        yc = y - mu
        var = _dot(bf(yc * yc), bdb) * (1.0 / HEAD_DIM)
        ksum = kd_s[0, rows, :] + kd_s[1, rows, :]
        bonus = _dot(bf(r_s[rows, :] * ksum * rk_ref[...]), bdb) * v_s[rows, :]
        yn = yc * lax.rsqrt(var + GN_EPS) * lng_ref[...] + lnb_ref[...] + bonus
        o_ref[0, rows, :] = (yn * sg_s[rows, :]).astype(BF16)
        return carry

    lax.fori_loop(0, t_ // tp, epilogue, 0)


def _rwkv(h3, w, p, tabs):
    b_, t_, _ = h3.shape
    args = [h3, w, p["shift"], p["w0"], p["wup"], p["a0"], p["aup"], p["kk"], p["ka"], p["rk"], p["lng"],
            p["lnb"], tabs["bd"], tabs["bdb"], tabs["tri2"], tabs["strict"], tabs["incl"], tabs["eye"]]
    big = lambda: pltpu.VMEM((t_, GROUP_W), F32)
    big2 = lambda: pltpu.VMEM((2, t_, GROUP_W), F32)
    return pl.pallas_call(
        functools.partial(_rwkv_kernel, t_=t_, tr=min(128, t_)),
        grid=(b_,),
        in_specs=[pl.BlockSpec((1, t_, D_MODEL), lambda b: (b, 0, 0))] + [_full(a) for a in args[1:]],
        out_specs=pl.BlockSpec((1, t_, GROUP_W), lambda b: (b, 0, 0)),
        out_shape=jax.ShapeDtypeStruct((b_, t_, GROUP_W), BF16),
        scratch_shapes=[pltpu.VMEM((t_ + 2 * HALO, RWKV_SHIFT_W), F32),
                        big(), big(), big(), big2(), big2(), big2(), big2(), big2(),
                        pltpu.VMEM((2, GROUP_W, GROUP_W), F32)],
        compiler_params=_params(),
        name="rwkv",
    )(*args)


def _prep_layer(l, P):
    w_in = P["w_in"][l].astype(BF16)
    c0, c1, c2 = A_W, A_W + B_W, A_W + B_W + C_W
    pad = jnp.zeros((2, RWKV_DECAY_RANK, GROUP_W), F32)
    wup = jnp.concatenate([P["rwkv_w_up"][l], pad], axis=1)
    aup = jnp.concatenate([pad, P["rwkv_a_up"][l]], axis=1)
    cat2 = lambda m: jnp.concatenate([m[0], m[1]], axis=1).astype(BF16)
    gw = P["lru_gate_w"][l]
    blocks = []
    for d in range(2):
        for kk in range(2):
            blocks.append(jax.scipy.linalg.block_diag(*[gw[d, kk, n] for n in range(LRU_BLOCKS)]))
    row = lambda a: a.reshape(1, -1)
    return dict(
        wA=w_in[:, :c0], wB=w_in[:, c0:c1], wC=w_in[:, c1:c2], wD=w_in[:, c2:],
        w_out=P["w_out"][l].astype(BF16),
        rwkv=dict(shift=P["rwkv_shift"][l], w0=P["rwkv_w0"][l], wup=cat2(wup), a0=P["rwkv_a0"][l],
                  aup=cat2(aup), kk=row(P["rwkv_k_k"][l]), ka=row(P["rwkv_k_a"][l]),
                  rk=row(P["rwkv_r_k"][l]), lng=row(P["rwkv_ln_g"][l]), lnb=row(P["rwkv_ln_b"][l])),
        qw=jnp.tile(P["attn_q_norm"][l], 2).reshape(1, -1), kw=jnp.tile(P["attn_k_norm"][l], 2).reshape(1, -1),
        cw=P["lru_conv_w"][l], cb=row(P["lru_conv_b"][l]),
        gw=jnp.concatenate(blocks, axis=1).astype(BF16), gb=P["lru_gate_b"][l].reshape(1, -1),
        lam=P["lru_lambda"][l], sink=P["swa_sink"][l],
    )


def _tables(t_):
    cos, sin = _rope_tables(t_)
    qi = np.arange(Q_BLOCK)[:, None]
    ki = np.arange(3 * Q_BLOCK)[None, :]
    dist = np.abs(ki - WINDOW - qi).astype(np.float32)
    slopes = np.exp2(-8.0 * np.arange(1, N_HEADS_G + 1, dtype=np.float32) / N_HEADS_G).astype(np.float32)
    alibi = np.where(dist[None] <= WINDOW, -slopes[:, None, None] * dist[None], NEG).astype(np.float32)
    first = np.where(ki[None] < Q_BLOCK, NEG, alibi).astype(np.float32)
    last = np.where(ki[None] >= 2 * Q_BLOCK, NEG, alibi).astype(np.float32)
    wbias = np.ascontiguousarray(np.stack([alibi, first, last]).transpose(0, 1, 3, 2))
    tabs = dict(cos=cos, sin=sin, seg128=_head_blockdiag()[:128, :128], wbias=wbias)
    tabs.update(_rwkv_tables())
    out = {k: jnp.asarray(v) for k, v in tabs.items()}
    for k in ("bdb", "tri2", "seg128"):
        out[k] = out[k].astype(BF16)
    return out


def _trunk(x, P, layers):
    b_, t_, _ = x.shape
    tabs = _tables(t_)
    x2 = x.reshape(b_ * t_, D_MODEL)
    h2 = _norm(x2, P["norm_g"][0].reshape(1, -1))
    for l in range(DEPTH):
        lp = layers[l]
        h3 = h2.reshape(b_, t_, D_MODEL)
        oa = _rwkv(h3, lp["wA"], lp["rwkv"], tabs)
        ob = _gattn(h3, lp["wB"], lp["qw"], lp["kw"], tabs)
        oc = _lru(h3, lp["wC"], lp["cw"], lp["cb"], lp["gw"], lp["gb"], lp["lam"])
        od = _wattn(h3, lp["wD"], lp["sink"], tabs)
        flat = lambda o: o.reshape(b_ * t_, GROUP_W)
        final = l == DEPTH - 1
        g_next = (P["final_g"] if final else P["norm_g"][l + 1]).reshape(1, -1)
        outs = _out_proj(flat(oa), flat(ob), flat(oc), flat(od), lp["w_out"], x2, g_next, final)
        if final:
            x2 = outs[0]
        else:
            x2, h2 = outs
    return x2.reshape(b_, t_, D_MODEL)


def kernel(x_prompt, x_sample, norm_g, w_in, w_out, rwkv_shift, rwkv_w0, rwkv_w_up, rwkv_a0, rwkv_a_up,
           rwkv_k_k, rwkv_k_a, rwkv_r_k, rwkv_ln_g, rwkv_ln_b, attn_q_norm, attn_k_norm, lru_conv_w,
           lru_conv_b, lru_gate_w, lru_gate_b, lru_lambda, swa_sink, final_g):
    P = dict(norm_g=norm_g, w_in=w_in, w_out=w_out, rwkv_shift=rwkv_shift, rwkv_w0=rwkv_w0,
             rwkv_w_up=rwkv_w_up, rwkv_a0=rwkv_a0, rwkv_a_up=rwkv_a_up, rwkv_k_k=rwkv_k_k,
             rwkv_k_a=rwkv_k_a, rwkv_r_k=rwkv_r_k, rwkv_ln_g=rwkv_ln_g, rwkv_ln_b=rwkv_ln_b,
             attn_q_norm=attn_q_norm, attn_k_norm=attn_k_norm, lru_conv_w=lru_conv_w,
             lru_conv_b=lru_conv_b, lru_gate_w=lru_gate_w, lru_gate_b=lru_gate_b,
             lru_lambda=lru_lambda, swa_sink=swa_sink, final_g=final_g)
    layers = [_prep_layer(l, P) for l in range(DEPTH)]
    return _trunk(x_prompt, P, layers), _trunk(x_sample, P, layers)
```

```python
import functools
import math

import jax
import jax.numpy as jnp
import numpy as np
from jax import lax
from jax.experimental import pallas as pl
from jax.experimental.pallas import tpu as pltpu

D_MODEL = 1024
DEPTH = 4
GRID_W = 64
HEAD_DIM = 64
GROUP_W = 256
N_HEADS_G = 4
N_KV = 2
REP = 2
KV_W = 128
RWKV_DECAY_RANK = 64
RWKV_SHIFT_W = 896
LRU_C = 8.0
LRU_BLOCKS = 4
LRU_BLK = 64
CONV_W = 4
CONV_LEFT = 2
Q_BLOCK = 128
WINDOW = 128
ROPE_THETA = 10000.0
NORM_EPS = 1e-6
GN_EPS = 64e-5
NEG = -1e30
A_W = 1152
B_W = 768
C_W = 512
D_W = 768

CHUNK = 64
RWKV_GROUP = 4
LRU_UNROLL = 4
GATTN_BLOCKS = 4
WATTN_BLOCKS = 8
ROW_TILE = 512
OUT_TILE = 1024
VT_ROWS = 80
HALO = 8
V7X_VMEM_LIMIT = 56 * 1024 * 1024
F32 = jnp.float32
BF16 = jnp.bfloat16
HI = lax.Precision.HIGHEST
EXP_M05 = math.exp(-0.5)
LOG2E = math.log2(math.e)


def _dot(a, b, prec=None):
    return jnp.dot(a, b, preferred_element_type=F32, precision=prec)


def _dot_nt(a, b, prec=None):
    return lax.dot_general(a, b, (((1,), (1,)), ((), ())), preferred_element_type=F32, precision=prec)


def _dot_tn(a, b, prec=None):
    return lax.dot_general(a, b, (((0,), (0,)), ((), ())), preferred_element_type=F32, precision=prec)


def _sigmoid(x):
    return jax.nn.sigmoid(x)


def _silu(x):
    return x * _sigmoid(x)


def _rms(x, g):
    return x * lax.rsqrt(jnp.mean(x * x, axis=-1, keepdims=True) + NORM_EPS) * g


def _params(n_axes=1):
    return pltpu.CompilerParams(dimension_semantics=("arbitrary",) * n_axes,
                                vmem_limit_bytes=V7X_VMEM_LIMIT)


def _full(a):
    nd = a.ndim
    return pl.BlockSpec(a.shape, lambda *_: (0,) * nd)


def _norm_kernel(x_ref, g_ref, h_ref):
    h_ref[...] = _rms(x_ref[...], g_ref[...]).astype(BF16)


def _norm(x2, g):
    m = x2.shape[0]
    tm = min(ROW_TILE, m)
    return pl.pallas_call(
        _norm_kernel,
        grid=(m // tm,),
        in_specs=[pl.BlockSpec((tm, D_MODEL), lambda i: (i, 0)), _full(g)],
        out_specs=pl.BlockSpec((tm, D_MODEL), lambda i: (i, 0)),
        out_shape=jax.ShapeDtypeStruct((m, D_MODEL), BF16),
        compiler_params=_params(),
        name="norm",
    )(x2, g)


def _out_kernel(oa_ref, ob_ref, oc_ref, od_ref, w_ref, x_ref, g_ref, *out_refs, final):
    acc = _dot(oa_ref[...], w_ref[0:GROUP_W, :])
    acc += _dot(ob_ref[...], w_ref[GROUP_W:2 * GROUP_W, :])
    acc += _dot(oc_ref[...], w_ref[2 * GROUP_W:3 * GROUP_W, :])
    acc += _dot(od_ref[...], w_ref[3 * GROUP_W:4 * GROUP_W, :])
    xn = x_ref[...] + acc
    if final:
        out_refs[0][...] = _rms(xn, g_ref[...])
    else:
        out_refs[0][...] = xn
        out_refs[1][...] = _rms(xn, g_ref[...]).astype(BF16)


def _out_proj(oa, ob, oc, od, w, x2, g, final):
    m = x2.shape[0]
    tm = min(OUT_TILE, m)
    ospec = pl.BlockSpec((tm, GROUP_W), lambda i: (i, 0))
    xspec = pl.BlockSpec((tm, D_MODEL), lambda i: (i, 0))
    if final:
        out_shape = (jax.ShapeDtypeStruct((m, D_MODEL), F32),)
        out_specs = (xspec,)
    else:
        out_shape = (jax.ShapeDtypeStruct((m, D_MODEL), F32), jax.ShapeDtypeStruct((m, D_MODEL), BF16))
        out_specs = (xspec, xspec)
    return pl.pallas_call(
        functools.partial(_out_kernel, final=final),
        grid=(m // tm,),
        in_specs=[ospec, ospec, ospec, ospec, _full(w), xspec, _full(g)],
        out_specs=out_specs,
        out_shape=out_shape,
        compiler_params=_params(),
        name="out_proj",
    )(oa, ob, oc, od, w, x2, g)


def _head_blockdiag():
    i = np.arange(GROUP_W)
    return (i[:, None] // HEAD_DIM == i[None, :] // HEAD_DIM).astype(np.float32)


def _rope_tables(t_):
    rows = t_ // GRID_W
    row = np.repeat(np.arange(rows), GRID_W).astype(np.float32)
    col = np.tile(np.arange(GRID_W), rows).astype(np.float32)
    half = HEAD_DIM // 2
    inv = (ROPE_THETA ** (-np.arange(0, half, 2, dtype=np.float32) / half)).astype(np.float32)
    ang_r = row[:, None] * inv
    ang_c = col[:, None] * inv
    cos = np.concatenate([np.cos(ang_r), np.cos(ang_r), np.cos(ang_c), np.cos(ang_c)], -1)
    sin = np.concatenate([-np.sin(ang_r), np.sin(ang_r), -np.sin(ang_c), np.sin(ang_c)], -1)
    return (np.tile(cos, (1, 2)).astype(np.float32), np.tile(sin, (1, 2)).astype(np.float32))


def _swap16(x):
    n = x.shape[-1]
    up = pltpu.roll(x, n - 16, axis=1)
    dn = pltpu.roll(x, 16, axis=1)
    lane = lax.broadcasted_iota(jnp.int32, x.shape, 1)
    return jnp.where((lane % 32) < 16, up, dn)


def _place_heads(half0, half1):
    lo = lax.broadcasted_iota(jnp.int32, half0.shape, 1) < HEAD_DIM
    z = jnp.zeros_like(half0)
    return [jnp.where(lo, half0, z), jnp.where(lo, pltpu.roll(half0, HEAD_DIM, axis=1), z),
            jnp.where(lo, z, pltpu.roll(half1, HEAD_DIM, axis=1)), jnp.where(lo, z, half1)]


def _gattn_kernel(h_ref, w_ref, qw_ref, kw_ref, cos_ref, sin_ref, seg_ref, o_ref,
                  qh_s, k_s, vt_s, sg_s, *, t_, tq, tr):
    seg = seg_ref[...]

    def prologue(i, carry):
        r0 = pl.multiple_of(i * tr, tr)
        rows = pl.ds(r0, tr)
        h = h_ref[0, rows, :]
        cos = cos_ref[rows, :]
        sin = sin_ref[rows, :]

        def normrope(z, wgt):
            ms = _dot((z * z).astype(BF16), seg) * (1.0 / HEAD_DIM)
            z = z * lax.rsqrt(ms + NORM_EPS) * wgt
            return z * cos + _swap16(z) * sin

        z = _dot(h, w_ref[...])
        sg_s[rows, :] = _silu(z[:, GROUP_W + 2 * KV_W:])
        halves = [normrope(z[:, c * 128:(c + 1) * 128], qw_ref[...]) * (HEAD_DIM ** -0.5 * LOG2E)
                  for c in range(2)]
        for hd, qm in enumerate(_place_heads(*halves)):
            qh_s[hd, rows, :] = qm.astype(BF16)
        k_s[rows, :] = normrope(z[:, GROUP_W:GROUP_W + KV_W], kw_ref[...]).astype(BF16)
        vt = z[:, GROUP_W + KV_W:GROUP_W + 2 * KV_W].T.astype(BF16)
        ones = jnp.ones((VT_ROWS - HEAD_DIM, tr), BF16)
        for kv in range(N_KV):
            vt_s[i, kv * VT_ROWS:kv * VT_ROWS + HEAD_DIM, :] = vt[kv * HEAD_DIM:(kv + 1) * HEAD_DIM, :]
            vt_s[i, kv * VT_ROWS + HEAD_DIM:(kv + 1) * VT_ROWS, :] = ones
        return carry

    lax.fori_loop(0, t_ // tr, prologue, 0)
    nvc = t_ // tr

    nqb = t_ // tq
    gb = min(GATTN_BLOCKS, nqb)

    def qblocks(i, carry):
        rows = [pl.ds(pl.multiple_of((i * gb + j) * tq, tq), tq) for j in range(gb)]
        units = [(j, pair) for j in range(gb) for pair in ((0, 1), (2, 3))]
        scores = lambda u: [_dot_nt(k_s[...], qh_s[hd, rows[u[0]], :]) for hd in u[1]]
        outs = [[] for _ in range(gb)]
        s_next = scores(units[0])
        for ui, (j, pair) in enumerate(units):
            ss = s_next
            if ui + 1 < len(units):
                s_next = scores(units[ui + 1])
            st = [dict(m=jnp.full((1, tq), NEG, F32), ot=jnp.zeros((VT_ROWS, tq), F32)) for _ in pair]
            for c in range(nvc):
                for k, hd in enumerate(pair):
                    g0 = (hd // REP) * VT_ROWS
                    a = st[k]
                    sc = ss[k][c * tr:(c + 1) * tr, :]
                    mc = jnp.maximum(a["m"], jnp.max(sc, axis=0, keepdims=True))
                    alpha = jnp.exp2(a["m"] - mc)
                    p = jnp.exp2(sc - mc)
                    a["ot"] = a["ot"] * alpha + _dot(vt_s[c, g0:g0 + VT_ROWS, :], p.astype(BF16))
                    a["m"] = mc
            outs[j] += [a["ot"][:HEAD_DIM, :] * (1.0 / a["ot"][HEAD_DIM:HEAD_DIM + 1, :]) for a in st]
        for j in range(gb):
            o = jnp.concatenate(outs[j], axis=0).T
            o_ref[0, rows[j], :] = (o * sg_s[rows[j], :]).astype(BF16)
        return carry

    lax.fori_loop(0, nqb // gb, qblocks, 0)


def _gattn(h3, w, qw, kw, tabs):
    b_, t_, _ = h3.shape
    tq = min(256, t_)
    tr = min(ROW_TILE, t_)
    cos, sin, seg = tabs["cos"], tabs["sin"], tabs["seg128"]
    return pl.pallas_call(
        functools.partial(_gattn_kernel, t_=t_, tq=tq, tr=tr),
        grid=(b_,),
        in_specs=[pl.BlockSpec((1, t_, D_MODEL), lambda b: (b, 0, 0)), _full(w), _full(qw), _full(kw),
                  _full(cos), _full(sin), _full(seg)],
        out_specs=pl.BlockSpec((1, t_, GROUP_W), lambda b: (b, 0, 0)),
        out_shape=jax.ShapeDtypeStruct((b_, t_, GROUP_W), BF16),
        scratch_shapes=[pltpu.VMEM((N_HEADS_G, t_, KV_W), BF16),
                        pltpu.VMEM((t_, KV_W), BF16),
                        pltpu.VMEM((t_ // tr, N_KV * VT_ROWS, tr), BF16),
                        pltpu.VMEM((t_, GROUP_W), F32)],
        compiler_params=_params(),
        name="gattn",
    )(h3, w, qw, kw, cos, sin, seg)


def _wattn_kernel(sink_ref, h_ref, w_ref, bias_ref, o_ref, qh_s, kp_s, vt_s, sg_s, *, t_, tr):
    nb = t_ // Q_BLOCK
    zblk = jnp.zeros((Q_BLOCK, KV_W), BF16)
    kp_s[0:WINDOW, :] = zblk
    kp_s[WINDOW + t_:, :] = zblk
    vt_s[0] = zblk
    vt_s[nb + 1] = zblk
    bpt = tr // Q_BLOCK

    def prologue(i, carry):
        r0 = pl.multiple_of(i * tr, tr)
        rows = pl.ds(r0, tr)
        h = h_ref[0, rows, :]
        z = _dot(h, w_ref[...])
        sg_s[rows, :] = _silu(z[:, GROUP_W + 2 * KV_W:])
        zq = z[:, 0:GROUP_W] * (HEAD_DIM ** -0.5)
        for hd, qm in enumerate(_place_heads(zq[:, 0:KV_W], zq[:, KV_W:])):
            qh_s[hd, rows, :] = qm.astype(BF16)
        kp_s[pl.ds(pl.multiple_of(WINDOW + r0, WINDOW), tr), :] = z[:, GROUP_W:GROUP_W + KV_W].astype(BF16)
        zv = z[:, GROUP_W + KV_W:GROUP_W + 2 * KV_W]
        for jb in range(bpt):
            vt_s[1 + i * bpt + jb] = zv[jb * Q_BLOCK:(jb + 1) * Q_BLOCK, :].T.astype(BF16)
        return carry

    lax.fori_loop(0, t_ // tr, prologue, 0)

    kw = 3 * Q_BLOCK
    npb = min(WATTN_BLOCKS, nb)

    def qblocks(i, carry):
        items = []
        for j in range(npb):
            blk = i * npb + j
            r0 = pl.multiple_of(blk * Q_BLOCK, Q_BLOCK)
            var = jnp.where(blk == 0, 1, jnp.where(blk == nb - 1, 2, 0))
            kwin = kp_s[pl.ds(r0, kw), :]
            for hd in range(N_HEADS_G):
                items.append((j, hd, blk, var, _dot_nt(kwin, qh_s[hd, pl.ds(r0, Q_BLOCK), :])))
        outs = [[None] * N_HEADS_G for _ in range(npb)]
        for j, hd, blk, var, s in items:
            s = s + bias_ref[var, hd]
            sk = sink_ref[hd]
            m = jnp.maximum(jnp.max(s, axis=0, keepdims=True), sk)
            p = jnp.exp(s - m)
            den = jnp.sum(p, axis=0, keepdims=True) + jnp.exp(sk - m)
            pb = p.astype(BF16)
            g0 = (hd // REP) * HEAD_DIM
            ot = _dot(vt_s[blk, g0:g0 + HEAD_DIM, :], pb[0:Q_BLOCK, :])
            for jj in range(1, 3):
                ot += _dot(vt_s[blk + jj, g0:g0 + HEAD_DIM, :], pb[jj * Q_BLOCK:(jj + 1) * Q_BLOCK, :])
            outs[j][hd] = ot * (1.0 / den)
        for j in range(npb):
            r0 = pl.multiple_of((i * npb + j) * Q_BLOCK, Q_BLOCK)
            o = jnp.concatenate(outs[j], axis=0).T
            o_ref[0, pl.ds(r0, Q_BLOCK), :] = (o * sg_s[pl.ds(r0, Q_BLOCK), :]).astype(BF16)
        return carry

    lax.fori_loop(0, nb // npb, qblocks, 0)


def _wattn(h3, w, sink, tabs):
    b_, t_, _ = h3.shape
    bias = tabs["wbias"]
    nb = t_ // Q_BLOCK
    assert nb >= 2
    return pl.pallas_call(
        functools.partial(_wattn_kernel, t_=t_, tr=min(ROW_TILE, t_)),
        grid=(b_,),
        in_specs=[pl.BlockSpec(memory_space=pltpu.SMEM),
                  pl.BlockSpec((1, t_, D_MODEL), lambda b: (b, 0, 0)), _full(w), _full(bias)],
        out_specs=pl.BlockSpec((1, t_, GROUP_W), lambda b: (b, 0, 0)),
        out_shape=jax.ShapeDtypeStruct((b_, t_, GROUP_W), BF16),
        scratch_shapes=[pltpu.VMEM((N_HEADS_G, t_, KV_W), BF16),
                        pltpu.VMEM((t_ + 2 * WINDOW, KV_W), BF16),
                        pltpu.VMEM((nb + 2, KV_W, Q_BLOCK), BF16),
                        pltpu.VMEM((t_, GROUP_W), F32)],
        compiler_params=_params(),
        name="wattn",
    )(sink, h3, w, bias)


def _tile_scan(a, b, rev):
    sub = lax.broadcasted_iota(jnp.int32, a.shape, 0)
    for s in (1, 2, 4):
        if rev:
            a_sh = pltpu.roll(a, 8 - s, axis=0)
            b_sh = pltpu.roll(b, 8 - s, axis=0)
            ok = sub < 8 - s
        else:
            a_sh = pltpu.roll(a, s, axis=0)
            b_sh = pltpu.roll(b, s, axis=0)
            ok = sub >= s
        b = jnp.where(ok, a * b_sh + b, b)
        a = jnp.where(ok, a * a_sh, a)
    return a, b


def _lru_kernel(h_ref, w_ref, cw_ref, cb_ref, gw_ref, gb_ref, lam_ref, o_ref, xp_s, a_s, b_s, hs_s, sg_s,
                *, t_, tr):
    zhalo = jnp.zeros((HALO, GROUP_W), F32)
    xp_s[0:HALO, :] = zhalo
    xp_s[HALO + t_:, :] = zhalo

    tp = min(ROW_TILE, t_)

    def proj(i, carry):
        r0 = pl.multiple_of(i * tp, tp)
        z = _dot(h_ref[0, pl.ds(r0, tp), :], w_ref[...])
        xp_s[pl.ds(pl.multiple_of(HALO + r0, HALO), tp), :] = z[:, 0:GROUP_W]
        sg_s[pl.ds(r0, tp), :] = _silu(z[:, GROUP_W:])
        return carry

    lax.fori_loop(0, t_ // tp, proj, 0)

    lam = lam_ref[...]
    sp = jnp.maximum(-lam, 0.0) + jnp.log(1.0 + jnp.exp(-jnp.abs(lam)))
    nsp = -LRU_C * sp

    def gates(i, carry):
        r0 = pl.multiple_of(i * tr, tr)
        rows = pl.ds(r0, tr)
        win = xp_s[pl.ds(r0, tr + 2 * HALO), :]
        xc = cb_ref[...]
        nw = tr + 2 * HALO
        for j in range(CONV_W):
            off = j - CONV_LEFT
            tap = win if off == 0 else pltpu.roll(win, (-off) % nw, axis=0)
            xc = xc + cw_ref[j:j + 1, :] * tap[HALO:HALO + tr, :]
        g = _dot(xc.astype(BF16), gw_ref[...]) + gb_ref[...]
        for d in range(2):
            r = _sigmoid(g[:, (2 * d) * GROUP_W:(2 * d + 1) * GROUP_W])
            ig = _sigmoid(g[:, (2 * d + 1) * GROUP_W:(2 * d + 2) * GROUP_W])
            a = jnp.exp(r * nsp[d:d + 1, :])
            a_s[d, rows, :] = a
            b_s[d, rows, :] = jnp.sqrt(1.0 - a * a) * (ig * xc)
        return carry

    lax.fori_loop(0, t_ // tr, gates, 0)

    nt = t_ // 8

    def tile(i, carry):
        cf, cr = carry
        rf = pl.multiple_of(i * 8, 8)
        rr = pl.multiple_of((nt - 1 - i) * 8, 8)
        af, bf = _tile_scan(a_s[0, pl.ds(rf, 8), :], b_s[0, pl.ds(rf, 8), :], False)
        ar, br = _tile_scan(a_s[1, pl.ds(rr, 8), :], b_s[1, pl.ds(rr, 8), :], True)
        hf = bf + af * cf
        hr = br + ar * cr
        hs_s[0, pl.ds(rf, 8), :] = hf
        hs_s[1, pl.ds(rr, 8), :] = hr
        return hf[7:8, :], hr[0:1, :]

    z = jnp.zeros((1, GROUP_W), F32)
    lax.fori_loop(0, nt, tile, (z, z), unroll=LRU_UNROLL if nt % LRU_UNROLL == 0 else 1)

    def epilogue(i, carry):
        rows = pl.ds(pl.multiple_of(i * tp, tp), tp)
        o_ref[0, rows, :] = ((hs_s[0, rows, :] + hs_s[1, rows, :]) * sg_s[rows, :]).astype(BF16)
        return carry

    lax.fori_loop(0, t_ // tp, epilogue, 0)


def _lru(h3, w, cw, cb, gw, gb, lam):
    b_, t_, _ = h3.shape
    return pl.pallas_call(
        functools.partial(_lru_kernel, t_=t_, tr=min(256, t_)),
        grid=(b_,),
        in_specs=[pl.BlockSpec((1, t_, D_MODEL), lambda b: (b, 0, 0)), _full(w), _full(cw), _full(cb),
                  _full(gw), _full(gb), _full(lam)],
        out_specs=pl.BlockSpec((1, t_, GROUP_W), lambda b: (b, 0, 0)),
        out_shape=jax.ShapeDtypeStruct((b_, t_, GROUP_W), BF16),
        scratch_shapes=[pltpu.VMEM((t_ + 2 * HALO, GROUP_W), F32),
                        pltpu.VMEM((2, t_, GROUP_W), F32), pltpu.VMEM((2, t_, GROUP_W), F32),
                        pltpu.VMEM((2, t_, GROUP_W), F32), pltpu.VMEM((t_, GROUP_W), F32)],
        compiler_params=_params(),
        name="lru",
    )(h3, w, cw, cb, gw, gb, lam)


def _rwkv_tables():
    bd = _head_blockdiag()
    t = np.arange(CHUNK)
    tri = np.stack([(t[:, None] >= t[None, :]), (t[:, None] <= t[None, :])]).astype(np.float32)
    strict = np.stack([(t[:, None] > t[None, :]), (t[:, None] < t[None, :])]).astype(np.float32)
    incl = tri.copy()
    eye = np.eye(CHUNK, dtype=np.float32)
    wide = lambda m: np.tile(m, (1,) * (m.ndim - 1) + (N_HEADS_G,))
    return dict(bd=bd, bdb=bd, tri2=np.concatenate([tri, tri], axis=-1), strict=wide(strict), incl=wide(incl),
                eye=wide(eye))


def _rwkv_kernel(h_ref, w_ref, sh_ref, w0_ref, wup_ref, a0_ref, aup_ref, kk_ref, ka_ref, rk_ref,
                 lng_ref, lnb_ref, bd_ref, bdb_ref, tri_ref, strict_ref, incl_ref, eye_ref, o_ref,
                 zs_s, r_s, v_s, sg_s, lg_s, pt_s, qd_s, kd_s, y_s, st_s, *, t_, tr):
    bd = bd_ref[...]
    bdb = bdb_ref[...]
    g = GROUP_W
    sw = RWKV_SHIFT_W
    cc = CHUNK
    bf = lambda x: x.astype(BF16)

    zhalo = jnp.zeros((HALO, sw), F32)
    zs_s[0:HALO, :] = zhalo
    zs_s[HALO + t_:, :] = zhalo

    tp = min(ROW_TILE, t_)

    def proj(i, carry):
        r0 = pl.multiple_of(i * tp, tp)
        z = _dot(h_ref[0, pl.ds(r0, tp), :], w_ref[...])
        zs_s[pl.ds(pl.multiple_of(HALO + r0, HALO), tp), :] = z[:, 0:sw]
        sg_s[pl.ds(r0, tp), :] = _silu(z[:, sw:])
        return carry

    lax.fori_loop(0, t_ // tp, proj, 0)

    ka1 = 1.0 - ka_ref[...]
    sh_self = 1.0 - sh_ref[0:1, :] - sh_ref[1:2, :]

    def prep(i, carry):
        r0 = pl.multiple_of(i * tr, tr)
        rows = pl.ds(r0, tr)

        def mixed(c0, c1):
            win = zs_s[pl.ds(r0, tr + 2 * HALO), c0:c1]
            nw = tr + 2 * HALO
            x = win[HALO:HALO + tr, :]
            prev = pltpu.roll(win, 1, axis=0)[HALO:HALO + tr, :]
            nxt = pltpu.roll(win, nw - 1, axis=0)[HALO:HALO + tr, :]
            return (sh_self[:, c0:c1] * x + sh_ref[0:1, c0:c1] * prev) + sh_ref[1:2, c0:c1] * nxt

        r_s[rows, :] = mixed(0, g)
        v_s[rows, :] = mixed(2 * g, 3 * g)
        k = mixed(g, 2 * g)
        lo = mixed(3 * g, sw)
        kk = k * kk_ref[...]
        ssq = _dot(bf(kk * kk), bdb)
        kk = kk * lax.rsqrt(jnp.maximum(ssq, 1e-24))
        dw = _dot(bf(jnp.tanh(lo)), wup_ref[...])
        da = _dot(bf(lo), aup_ref[...])
        for d in range(2):
            lw = -EXP_M05 * _sigmoid(w0_ref[d:d + 1, :] + dw[:, d * g:(d + 1) * g])
            a = _sigmoid(a0_ref[d:d + 1, :] + da[:, d * g:(d + 1) * g])
            kd_s[d, rows, :] = k * (ka1 + a * ka_ref[...])
            qd_s[d, rows, :] = kk * a
            lw_hi = bf(lw)
            lw_lo = bf(lw - lw_hi.astype(F32))
            lgs = [_dot(tri_ref[d], jnp.concatenate([lw_hi[c0:c0 + cc, :], lw_lo[c0:c0 + cc, :]], axis=0))
                   for c0 in range(0, tr, cc)]
            lg = jnp.concatenate(lgs, axis=0) if len(lgs) > 1 else lgs[0]
            lg_s[d, rows, :] = lg
            pt_s[d, rows, :] = kk * jnp.exp(lg - lw)
        return carry

    lax.fori_loop(0, t_ // tr, prep, 0)
    st_s[...] = jnp.zeros((2, g, g), F32)

    nc = t_ // cc
    ug = min(RWKV_GROUP, nc)
    rep = lambda x: jnp.concatenate([bf(x)] * N_HEADS_G, axis=0) * bdb
    stack = lambda a, b: bf(jnp.concatenate([a, b], axis=0))
    eye = eye_ref[...]

    def group(i, carry):
        ch = []
        for j in range(ug):
            for d in range(2):
                c = i * ug + j
                c = c if d == 0 else nc - 1 - c
                rows = pl.ds(pl.multiple_of(c * cc, cc), cc)
                lg = lg_s[d, rows, :]
                tot = lg[cc - 1:cc, :] if d == 0 else lg[0:1, :]
                g_inv = jnp.exp(-lg)
                g_end = jnp.exp(tot - lg)
                qdc = qd_s[d, rows, :]
                kdc = kd_s[d, rows, :]
                ch.append(dict(d=d, rows=rows, gtot=jnp.exp(tot), pt=pt_s[d, rows, :],
                               rt=r_s[rows, :] * jnp.exp(lg), qt=qdc * g_inv, kt=kdc * g_inv,
                               qe=qdc * g_end, ke=kdc * g_end, vc=v_s[rows, :]))
        for s in ch:
            pr = stack(s["pt"], s["rt"])
            a_q = _dot_nt(pr, rep(s["qt"]))
            a_k = _dot_nt(pr, rep(s["kt"]))
            strict = strict_ref[s["d"]]
            incl = incl_ref[s["d"]]
            s["a_rq"] = bf(a_q[cc:, :] * incl)
            s["a_k"] = stack(a_k[:cc, :] * strict, a_k[cc:, :] * incl)
            s["x"] = -(a_q[:cc, :] * strict)
            s["tm"] = eye + s["x"]
        for s in ch:
            s["x"] = _dot(bf(s["x"]), rep(s["x"]))
        for k in range(5):
            for s in ch:
                if k < 4:
                    both = _dot(stack(s["x"], s["tm"]), rep(s["x"]))
                    s["x"] = both[:cc, :]
                    s["tm"] = s["tm"] + both[cc:, :]
                else:
                    s["tm"] = s["tm"] + _dot(bf(s["tm"]), rep(s["x"]))
        for s in ch:
            s["tmb"] = bf(s["tm"])
            s["wm"] = _dot(s["tmb"], rep(s["pt"]))
            s["av"] = _dot(s["a_k"], rep(s["vc"]))
        for s in ch:
            s["u0"] = _dot(s["tmb"], rep(s["av"][:cc, :]))
            s["rm"] = bf(s["rt"] - _dot(s["a_rq"], rep(s["wm"])))
            s["mm"] = bf(_dot_tn(s["wm"], s["qe"]) * bd)
        for s in ch:
            s["y0"] = s["av"][cc:, :] - _dot(s["a_rq"], rep(s["u0"]))
            s["nn"] = _dot_tn(jnp.concatenate([s["vc"], -s["u0"]], axis=0),
                              jnp.concatenate([s["ke"], s["qe"]], axis=0)) * bd
        sts = [st_s[0], st_s[1]]
        for j in range(ug):
            for d in range(2):
                s = ch[2 * j + d]
                sb = bf(sts[d])
                upd = _dot(sb, s["mm"])
                y_s[d, s["rows"], :] = _dot_nt(s["rm"], sb) + s["y0"]
                sts[d] = sts[d] * s["gtot"] - upd + s["nn"]
        st_s[0] = sts[0]
        st_s[1] = sts[1]
        return carry

    lax.fori_loop(0, nc // ug, group, 0)

    def epilogue(i, carry):
        r0 = pl.multiple_of(i * tp, tp)
        rows = pl.ds(r0, tp)
        y = y_s[0, rows, :] + y_s[1, rows, :]
        mu = _dot(bf(y), bdb) * (1.0 / HEAD_DIM)
        yc = y - mu
        var = _dot(bf(yc * yc), bdb) * (1.0 / HEAD_DIM)
        ksum = kd_s[0, rows, :] + kd_s[1, rows, :]
        bonus = _dot(bf(r_s[rows, :] * ksum * rk_ref[...]), bdb) * v_s[rows, :]
        yn = yc * lax.rsqrt(var + GN_EPS) * lng_ref[...] + lnb_ref[...] + bonus
        o_ref[0, rows, :] = (yn * sg_s[rows, :]).astype(BF16)
        return carry

    lax.fori_loop(0, t_ // tp, epilogue, 0)


def _rwkv(h3, w, p, tabs):
    b_, t_, _ = h3.shape
    args = [h3, w, p["shift"], p["w0"], p["wup"], p["a0"], p["aup"], p["kk"], p["ka"], p["rk"], p["lng"],
            p["lnb"], tabs["bd"], tabs["bdb"], tabs["tri2"], tabs["strict"], tabs["incl"], tabs["eye"]]
    big = lambda: pltpu.VMEM((t_, GROUP_W), F32)
    big2 = lambda: pltpu.VMEM((2, t_, GROUP_W), F32)
    return pl.pallas_call(
        functools.partial(_rwkv_kernel, t_=t_, tr=min(128, t_)),
        grid=(b_,),
        in_specs=[pl.BlockSpec((1, t_, D_MODEL), lambda b: (b, 0, 0))] + [_full(a) for a in args[1:]],
        out_specs=pl.BlockSpec((1, t_, GROUP_W), lambda b: (b, 0, 0)),
        out_shape=jax.ShapeDtypeStruct((b_, t_, GROUP_W), BF16),
        scratch_shapes=[pltpu.VMEM((t_ + 2 * HALO, RWKV_SHIFT_W), F32),
                        big(), big(), big(), big2(), big2(), big2(), big2(), big2(),
                        pltpu.VMEM((2, GROUP_W, GROUP_W), F32)],
        compiler_params=_params(),
        name="rwkv",
    )(*args)


def _prep_layer(l, P):
    w_in = P["w_in"][l].astype(BF16)
    c0, c1, c2 = A_W, A_W + B_W, A_W + B_W + C_W
    pad = jnp.zeros((2, RWKV_DECAY_RANK, GROUP_W), F32)
    wup = jnp.concatenate([P["rwkv_w_up"][l], pad], axis=1)
    aup = jnp.concatenate([pad, P["rwkv_a_up"][l]], axis=1)
    cat2 = lambda m: jnp.concatenate([m[0], m[1]], axis=1).astype(BF16)
    gw = P["lru_gate_w"][l]
    blocks = []
    for d in range(2):
        for kk in range(2):
            blocks.append(jax.scipy.linalg.block_diag(*[gw[d, kk, n] for n in range(LRU_BLOCKS)]))
    row = lambda a: a.reshape(1, -1)
    return dict(
        wA=w_in[:, :c0], wB=w_in[:, c0:c1], wC=w_in[:, c1:c2], wD=w_in[:, c2:],
        w_out=P["w_out"][l].astype(BF16),
        rwkv=dict(shift=P["rwkv_shift"][l], w0=P["rwkv_w0"][l], wup=cat2(wup), a0=P["rwkv_a0"][l],
                  aup=cat2(aup), kk=row(P["rwkv_k_k"][l]), ka=row(P["rwkv_k_a"][l]),
                  rk=row(P["rwkv_r_k"][l]), lng=row(P["rwkv_ln_g"][l]), lnb=row(P["rwkv_ln_b"][l])),
        qw=jnp.tile(P["attn_q_norm"][l], 2).reshape(1, -1), kw=jnp.tile(P["attn_k_norm"][l], 2).reshape(1, -1),
        cw=P["lru_conv_w"][l], cb=row(P["lru_conv_b"][l]),
        gw=jnp.concatenate(blocks, axis=1).astype(BF16), gb=P["lru_gate_b"][l].reshape(1, -1),
        lam=P["lru_lambda"][l], sink=P["swa_sink"][l],
    )


def _tables(t_):
    cos, sin = _rope_tables(t_)
    qi = np.arange(Q_BLOCK)[:, None]
    ki = np.arange(3 * Q_BLOCK)[None, :]
    dist = np.abs(ki - WINDOW - qi).astype(np.float32)
    slopes = np.exp2(-8.0 * np.arange(1, N_HEADS_G + 1, dtype=np.float32) / N_HEADS_G).astype(np.float32)
    alibi = np.where(dist[None] <= WINDOW, -slopes[:, None, None] * dist[None], NEG).astype(np.float32)
    first = np.where(ki[None] < Q_BLOCK, NEG, alibi).astype(np.float32)
    last = np.where(ki[None] >= 2 * Q_BLOCK, NEG, alibi).astype(np.float32)
    wbias = np.ascontiguousarray(np.stack([alibi, first, last]).transpose(0, 1, 3, 2))
    tabs = dict(cos=cos, sin=sin, seg128=_head_blockdiag()[:128, :128], wbias=wbias)
    tabs.update(_rwkv_tables())
    out = {k: jnp.asarray(v) for k, v in tabs.items()}
    for k in ("bdb", "tri2", "seg128"):
        out[k] = out[k].astype(BF16)
    return out


def _trunk(x, P, layers):
    b_, t_, _ = x.shape
    tabs = _tables(t_)
    x2 = x.reshape(b_ * t_, D_MODEL)
    h2 = _norm(x2, P["norm_g"][0].reshape(1, -1))
    for l in range(DEPTH):
        lp = layers[l]
        h3 = h2.reshape(b_, t_, D_MODEL)
        oa = _rwkv(h3, lp["wA"], lp["rwkv"], tabs)
        ob = _gattn(h3, lp["wB"], lp["qw"], lp["kw"], tabs)
        oc = _lru(h3, lp["wC"], lp["cw"], lp["cb"], lp["gw"], lp["gb"], lp["lam"])
        od = _wattn(h3, lp["wD"], lp["sink"], tabs)
        flat = lambda o: o.reshape(b_ * t_, GROUP_W)
        final = l == DEPTH - 1
        g_next = (P["final_g"] if final else P["norm_g"][l + 1]).reshape(1, -1)
        outs = _out_proj(flat(oa), flat(ob), flat(oc), flat(od), lp["w_out"], x2, g_next, final)
        if final:
            x2 = outs[0]
        else:
            x2, h2 = outs
    return x2.reshape(b_, t_, D_MODEL)


def kernel(x_prompt, x_sample, norm_g, w_in, w_out, rwkv_shift, rwkv_w0, rwkv_w_up, rwkv_a0, rwkv_a_up,
           rwkv_k_k, rwkv_k_a, rwkv_r_k, rwkv_ln_g, rwkv_ln_b, attn_q_norm, attn_k_norm, lru_conv_w,
           lru_conv_b, lru_gate_w, lru_gate_b, lru_lambda, swa_sink, final_g):
    P = dict(norm_g=norm_g, w_in=w_in, w_out=w_out, rwkv_shift=rwkv_shift, rwkv_w0=rwkv_w0,
             rwkv_w_up=rwkv_w_up, rwkv_a0=rwkv_a0, rwkv_a_up=rwkv_a_up, rwkv_k_k=rwkv_k_k,
             rwkv_k_a=rwkv_k_a, rwkv_r_k=rwkv_r_k, rwkv_ln_g=rwkv_ln_g, rwkv_ln_b=rwkv_ln_b,
             attn_q_norm=attn_q_norm, attn_k_norm=attn_k_norm, lru_conv_w=lru_conv_w,
             lru_conv_b=lru_conv_b, lru_gate_w=lru_gate_w, lru_gate_b=lru_gate_b,
             lru_lambda=lru_lambda, swa_sink=swa_sink, final_g=final_g)
    layers = [_prep_layer(l, P) for l in range(DEPTH)]
    return _trunk(x_prompt, P, layers), _trunk(x_sample, P, layers)
```

```python
import functools
import math

import jax
import jax.numpy as jnp
import numpy as np
from jax import lax
from jax.experimental import pallas as pl
from jax.experimental.pallas import tpu as pltpu

D_MODEL = 1024
DEPTH = 4
GRID_W = 64
HEAD_DIM = 64
GROUP_W = 256
N_HEADS_G = 4
N_KV = 2
REP = 2
KV_W = 128
RWKV_DECAY_RANK = 64
RWKV_SHIFT_W = 896
LRU_C = 8.0
LRU_BLOCKS = 4
LRU_BLK = 64
CONV_W = 4
CONV_LEFT = 2
Q_BLOCK = 128
WINDOW = 128
ROPE_THETA = 10000.0
NORM_EPS = 1e-6
GN_EPS = 64e-5
NEG = -1e30
A_W = 1152
B_W = 768
C_W = 512
D_W = 768

CHUNK = 64
RWKV_GROUP = 4
LRU_UNROLL = 4
GATTN_BLOCKS = 4
WATTN_BLOCKS = 16
ROW_TILE = 512
OUT_TILE = 1024
VT_ROWS = 80
HALO = 8
V7X_VMEM_LIMIT = 56 * 1024 * 1024
F32 = jnp.float32
BF16 = jnp.bfloat16
EXP_M05 = math.exp(-0.5)
LOG2E = math.log2(math.e)


def _dot(a, b):
    return jnp.dot(a, b, preferred_element_type=F32)


def _dot_nt(a, b):
    return lax.dot_general(a, b, (((1,), (1,)), ((), ())), preferred_element_type=F32)


def _dot_tn(a, b):
    return lax.dot_general(a, b, (((0,), (0,)), ((), ())), preferred_element_type=F32)


def _sigmoid(x):
    return jax.nn.sigmoid(x)


def _silu(x):
    return x * _sigmoid(x)


def _rms(x, g):
    return x * lax.rsqrt(jnp.mean(x * x, axis=-1, keepdims=True) + NORM_EPS) * g


def _params(n_axes=1):
    return pltpu.CompilerParams(dimension_semantics=("arbitrary",) * n_axes,
                                vmem_limit_bytes=V7X_VMEM_LIMIT)


def _full(a):
    nd = a.ndim
    return pl.BlockSpec(a.shape, lambda *_: (0,) * nd)


def _norm_kernel(x_ref, g_ref, h_ref):
    h_ref[...] = _rms(x_ref[...], g_ref[...]).astype(BF16)


def _norm(x2, g):
    m = x2.shape[0]
    tm = min(ROW_TILE, m)
    return pl.pallas_call(
        _norm_kernel,
        grid=(m // tm,),
        in_specs=[pl.BlockSpec((tm, D_MODEL), lambda i: (i, 0)), _full(g)],
        out_specs=pl.BlockSpec((tm, D_MODEL), lambda i: (i, 0)),
        out_shape=jax.ShapeDtypeStruct((m, D_MODEL), BF16),
        compiler_params=_params(),
        name="norm",
    )(x2, g)


def _out_kernel(oa_ref, ob_ref, oc_ref, od_ref, w_ref, x_ref, g_ref, *out_refs, final):
    acc = _dot(oa_ref[...], w_ref[0:GROUP_W, :])
    acc += _dot(ob_ref[...], w_ref[GROUP_W:2 * GROUP_W, :])
    acc += _dot(oc_ref[...], w_ref[2 * GROUP_W:3 * GROUP_W, :])
    acc += _dot(od_ref[...], w_ref[3 * GROUP_W:4 * GROUP_W, :])
    xn = x_ref[...] + acc
    if final:
        out_refs[0][...] = _rms(xn, g_ref[...])
    else:
        out_refs[0][...] = xn
        out_refs[1][...] = _rms(xn, g_ref[...]).astype(BF16)


def _out_proj(oa, ob, oc, od, w, x2, g, final):
    m = x2.shape[0]
    tm = min(OUT_TILE, m)
    ospec = pl.BlockSpec((tm, GROUP_W), lambda i: (i, 0))
    xspec = pl.BlockSpec((tm, D_MODEL), lambda i: (i, 0))
    if final:
        out_shape = (jax.ShapeDtypeStruct((m, D_MODEL), F32),)
        out_specs = (xspec,)
    else:
        out_shape = (jax.ShapeDtypeStruct((m, D_MODEL), F32), jax.ShapeDtypeStruct((m, D_MODEL), BF16))
        out_specs = (xspec, xspec)
    return pl.pallas_call(
        functools.partial(_out_kernel, final=final),
        grid=(m // tm,),
        in_specs=[ospec, ospec, ospec, ospec, _full(w), xspec, _full(g)],
        out_specs=out_specs,
        out_shape=out_shape,
        compiler_params=_params(),
        name="out_proj",
    )(oa, ob, oc, od, w, x2, g)


def _head_blockdiag():
    i = np.arange(GROUP_W)
    return (i[:, None] // HEAD_DIM == i[None, :] // HEAD_DIM).astype(np.float32)


def _rope_tables(t_):
    rows = t_ // GRID_W
    row = np.repeat(np.arange(rows), GRID_W).astype(np.float32)
    col = np.tile(np.arange(GRID_W), rows).astype(np.float32)
    half = HEAD_DIM // 2
    inv = (ROPE_THETA ** (-np.arange(0, half, 2, dtype=np.float32) / half)).astype(np.float32)
    ang_r = row[:, None] * inv
    ang_c = col[:, None] * inv
    cos = np.concatenate([np.cos(ang_r), np.cos(ang_r), np.cos(ang_c), np.cos(ang_c)], -1)
    sin = np.concatenate([-np.sin(ang_r), np.sin(ang_r), -np.sin(ang_c), np.sin(ang_c)], -1)
    return (np.tile(cos, (1, 2)).astype(np.float32), np.tile(sin, (1, 2)).astype(np.float32))


def _swap16(x):
    n = x.shape[-1]
    up = pltpu.roll(x, n - 16, axis=1)
    dn = pltpu.roll(x, 16, axis=1)
    lane = lax.broadcasted_iota(jnp.int32, x.shape, 1)
    return jnp.where((lane % 32) < 16, up, dn)


def _place_heads(half0, half1):
    lo = lax.broadcasted_iota(jnp.int32, half0.shape, 1) < HEAD_DIM
    z = jnp.zeros_like(half0)
    return [jnp.where(lo, half0, z), jnp.where(lo, pltpu.roll(half0, HEAD_DIM, axis=1), z),
            jnp.where(lo, z, pltpu.roll(half1, HEAD_DIM, axis=1)), jnp.where(lo, z, half1)]


def _gattn_kernel(h_ref, w_ref, qw_ref, kw_ref, cos_ref, sin_ref, seg_ref, o_ref,
                  qh_s, k_s, vt_s, sg_s, *, t_, tq, tr):
    seg = seg_ref[...]

    def prologue(i, carry):
        r0 = pl.multiple_of(i * tr, tr)
        rows = pl.ds(r0, tr)
        h = h_ref[0, rows, :]
        cos = cos_ref[rows, :]
        sin = sin_ref[rows, :]

        def normrope(z, wgt):
            ms = _dot((z * z).astype(BF16), seg) * (1.0 / HEAD_DIM)
            z = z * lax.rsqrt(ms + NORM_EPS) * wgt
            return z * cos + _swap16(z) * sin

        z = _dot(h, w_ref[...])
        sg_s[rows, :] = _silu(z[:, GROUP_W + 2 * KV_W:])
        halves = [normrope(z[:, c * 128:(c + 1) * 128], qw_ref[...]) * (HEAD_DIM ** -0.5 * LOG2E)
                  for c in range(2)]
        for hd, qm in enumerate(_place_heads(*halves)):
            qh_s[hd, rows, :] = qm.astype(BF16)
        k_s[rows, :] = normrope(z[:, GROUP_W:GROUP_W + KV_W], kw_ref[...]).astype(BF16)
        vt = z[:, GROUP_W + KV_W:GROUP_W + 2 * KV_W].T.astype(BF16)
        ones = jnp.ones((VT_ROWS - HEAD_DIM, tr), BF16)
        for kv in range(N_KV):
            vt_s[i, kv * VT_ROWS:kv * VT_ROWS + HEAD_DIM, :] = vt[kv * HEAD_DIM:(kv + 1) * HEAD_DIM, :]
            vt_s[i, kv * VT_ROWS + HEAD_DIM:(kv + 1) * VT_ROWS, :] = ones
        return carry

    lax.fori_loop(0, t_ // tr, prologue, 0)
    nvc = t_ // tr

    nqb = t_ // tq
    gb = min(GATTN_BLOCKS, nqb)

    def qblocks(i, carry):
        rows = [pl.ds(pl.multiple_of((i * gb + j) * tq, tq), tq) for j in range(gb)]
        units = [(j, pair) for j in range(gb) for pair in ((0, 1), (2, 3))]
        scores = lambda u: [_dot_nt(k_s[...], qh_s[hd, rows[u[0]], :]) for hd in u[1]]
        outs = [[] for _ in range(gb)]
        s_next = scores(units[0])
        for ui, (j, pair) in enumerate(units):
            ss = s_next
            if ui + 1 < len(units):
                s_next = scores(units[ui + 1])
            st = [dict(m=jnp.full((1, tq), NEG, F32), ot=jnp.zeros((VT_ROWS, tq), F32)) for _ in pair]
            for c in range(nvc):
                for k, hd in enumerate(pair):
                    g0 = (hd // REP) * VT_ROWS
                    a = st[k]
                    sc = ss[k][c * tr:(c + 1) * tr, :]
                    mc = jnp.maximum(a["m"], jnp.max(sc, axis=0, keepdims=True))
                    alpha = jnp.exp2(a["m"] - mc)
                    p = jnp.exp2(sc - mc)
                    a["ot"] = a["ot"] * alpha + _dot(vt_s[c, g0:g0 + VT_ROWS, :], p.astype(BF16))
                    a["m"] = mc
            outs[j] += [a["ot"][:HEAD_DIM, :] * (1.0 / a["ot"][HEAD_DIM:HEAD_DIM + 1, :]) for a in st]
        for j in range(gb):
            o = jnp.concatenate(outs[j], axis=0).T
            o_ref[0, rows[j], :] = (o * sg_s[rows[j], :]).astype(BF16)
        return carry

    lax.fori_loop(0, nqb // gb, qblocks, 0)


def _gattn(h3, w, qw, kw, tabs):
    b_, t_, _ = h3.shape
    tq = min(256, t_)
    tr = min(ROW_TILE, t_)
    cos, sin, seg = tabs["cos"], tabs["sin"], tabs["seg128"]
    return pl.pallas_call(
        functools.partial(_gattn_kernel, t_=t_, tq=tq, tr=tr),
        grid=(b_,),
        in_specs=[pl.BlockSpec((1, t_, D_MODEL), lambda b: (b, 0, 0)), _full(w), _full(qw), _full(kw),
                  _full(cos), _full(sin), _full(seg)],
        out_specs=pl.BlockSpec((1, t_, GROUP_W), lambda b: (b, 0, 0)),
        out_shape=jax.ShapeDtypeStruct((b_, t_, GROUP_W), BF16),
        scratch_shapes=[pltpu.VMEM((N_HEADS_G, t_, KV_W), BF16),
                        pltpu.VMEM((t_, KV_W), BF16),
                        pltpu.VMEM((t_ // tr, N_KV * VT_ROWS, tr), BF16),
                        pltpu.VMEM((t_, GROUP_W), F32)],
        compiler_params=_params(),
        name="gattn",
    )(h3, w, qw, kw, cos, sin, seg)


def _wattn_kernel(sink_ref, h_ref, w_ref, bias_ref, o_ref, qh_s, kp_s, vt_s, sg_s, *, t_, tr):
    nb = t_ // Q_BLOCK
    zblk = jnp.zeros((Q_BLOCK, KV_W), BF16)
    kp_s[0:WINDOW, :] = zblk
    kp_s[WINDOW + t_:, :] = zblk
    vt_s[0] = zblk
    vt_s[nb + 1] = zblk
    bpt = tr // Q_BLOCK

    def prologue(i, carry):
        r0 = pl.multiple_of(i * tr, tr)
        rows = pl.ds(r0, tr)
        h = h_ref[0, rows, :]
        z = _dot(h, w_ref[...])
        sg_s[rows, :] = _silu(z[:, GROUP_W + 2 * KV_W:])
        zq = z[:, 0:GROUP_W] * (HEAD_DIM ** -0.5)
        for hd, qm in enumerate(_place_heads(zq[:, 0:KV_W], zq[:, KV_W:])):
            qh_s[hd, rows, :] = qm.astype(BF16)
        kp_s[pl.ds(pl.multiple_of(WINDOW + r0, WINDOW), tr), :] = z[:, GROUP_W:GROUP_W + KV_W].astype(BF16)
        zv = z[:, GROUP_W + KV_W:GROUP_W + 2 * KV_W]
        for jb in range(bpt):
            vt_s[1 + i * bpt + jb] = zv[jb * Q_BLOCK:(jb + 1) * Q_BLOCK, :].T.astype(BF16)
        return carry

    lax.fori_loop(0, t_ // tr, prologue, 0)

    kw = 3 * Q_BLOCK
    npb = min(WATTN_BLOCKS, nb)

    def qblocks(i, carry):
        items = []
        for j in range(npb):
            blk = i * npb + j
            r0 = pl.multiple_of(blk * Q_BLOCK, Q_BLOCK)
            var = jnp.where(blk == 0, 1, jnp.where(blk == nb - 1, 2, 0))
            kwin = kp_s[pl.ds(r0, kw), :]
            for hd in range(N_HEADS_G):
                items.append((j, hd, blk, var, _dot_nt(kwin, qh_s[hd, pl.ds(r0, Q_BLOCK), :])))
        outs = [[None] * N_HEADS_G for _ in range(npb)]
        for j, hd, blk, var, s in items:
            s = s + bias_ref[var, hd]
            sk = sink_ref[hd]
            m = jnp.maximum(jnp.max(s, axis=0, keepdims=True), sk)
            p = jnp.exp(s - m)
            den = jnp.sum(p, axis=0, keepdims=True) + jnp.exp(sk - m)
            pb = p.astype(BF16)
            g0 = (hd // REP) * HEAD_DIM
            ot = _dot(vt_s[blk, g0:g0 + HEAD_DIM, :], pb[0:Q_BLOCK, :])
            for jj in range(1, 3):
                ot += _dot(vt_s[blk + jj, g0:g0 + HEAD_DIM, :], pb[jj * Q_BLOCK:(jj + 1) * Q_BLOCK, :])
            outs[j][hd] = ot * (1.0 / den)
        for j in range(npb):
            r0 = pl.multiple_of((i * npb + j) * Q_BLOCK, Q_BLOCK)
            o = jnp.concatenate(outs[j], axis=0).T
            o_ref[0, pl.ds(r0, Q_BLOCK), :] = (o * sg_s[pl.ds(r0, Q_BLOCK), :]).astype(BF16)
        return carry

    lax.fori_loop(0, nb // npb, qblocks, 0)


def _wattn(h3, w, sink, tabs):
    b_, t_, _ = h3.shape
    bias = tabs["wbias"]
    nb = t_ // Q_BLOCK
    assert nb >= 2
    return pl.pallas_call(
        functools.partial(_wattn_kernel, t_=t_, tr=min(ROW_TILE, t_)),
        grid=(b_,),
        in_specs=[pl.BlockSpec(memory_space=pltpu.SMEM),
                  pl.BlockSpec((1, t_, D_MODEL), lambda b: (b, 0, 0)), _full(w), _full(bias)],
        out_specs=pl.BlockSpec((1, t_, GROUP_W), lambda b: (b, 0, 0)),
        out_shape=jax.ShapeDtypeStruct((b_, t_, GROUP_W), BF16),
        scratch_shapes=[pltpu.VMEM((N_HEADS_G, t_, KV_W), BF16),
                        pltpu.VMEM((t_ + 2 * WINDOW, KV_W), BF16),
                        pltpu.VMEM((nb + 2, KV_W, Q_BLOCK), BF16),
                        pltpu.VMEM((t_, GROUP_W), F32)],
        compiler_params=_params(),
        name="wattn",
    )(sink, h3, w, bias)


def _tile_scan(a, b, rev):
    sub = lax.broadcasted_iota(jnp.int32, a.shape, 0)
    for s in (1, 2, 4):
        if rev:
            a_sh = pltpu.roll(a, 8 - s, axis=0)
            b_sh = pltpu.roll(b, 8 - s, axis=0)
            ok = sub < 8 - s
        else:
            a_sh = pltpu.roll(a, s, axis=0)
            b_sh = pltpu.roll(b, s, axis=0)
            ok = sub >= s
        b = jnp.where(ok, a * b_sh + b, b)
        a = jnp.where(ok, a * a_sh, a)
    return a, b


def _lru_kernel(h_ref, w_ref, cw_ref, cb_ref, gw_ref, gb_ref, lam_ref, o_ref, xp_s, a_s, b_s, hs_s, sg_s,
                *, t_, tr):
    zhalo = jnp.zeros((HALO, GROUP_W), F32)
    xp_s[0:HALO, :] = zhalo
    xp_s[HALO + t_:, :] = zhalo

    tp = min(ROW_TILE, t_)

    def proj(i, carry):
        r0 = pl.multiple_of(i * tp, tp)
        z = _dot(h_ref[0, pl.ds(r0, tp), :], w_ref[...])
        xp_s[pl.ds(pl.multiple_of(HALO + r0, HALO), tp), :] = z[:, 0:GROUP_W]
        sg_s[pl.ds(r0, tp), :] = _silu(z[:, GROUP_W:])
        return carry

    lax.fori_loop(0, t_ // tp, proj, 0)

    lam = lam_ref[...]
    sp = jnp.maximum(-lam, 0.0) + jnp.log(1.0 + jnp.exp(-jnp.abs(lam)))
    nsp = -LRU_C * sp

    def gates(i, carry):
        r0 = pl.multiple_of(i * tr, tr)
        rows = pl.ds(r0, tr)
        win = xp_s[pl.ds(r0, tr + 2 * HALO), :]
        xc = cb_ref[...]
        nw = tr + 2 * HALO
        for j in range(CONV_W):
            off = j - CONV_LEFT
            tap = win if off == 0 else pltpu.roll(win, (-off) % nw, axis=0)
            xc = xc + cw_ref[j:j + 1, :] * tap[HALO:HALO + tr, :]
        g = _dot(xc.astype(BF16), gw_ref[...]) + gb_ref[...]
        for d in range(2):
            r = _sigmoid(g[:, (2 * d) * GROUP_W:(2 * d + 1) * GROUP_W])
            ig = _sigmoid(g[:, (2 * d + 1) * GROUP_W:(2 * d + 2) * GROUP_W])
            a = jnp.exp(r * nsp[d:d + 1, :])
            a_s[d, rows, :] = a
            b_s[d, rows, :] = jnp.sqrt(1.0 - a * a) * (ig * xc)
        return carry

    lax.fori_loop(0, t_ // tr, gates, 0)

    nt = t_ // 8

    def tile(i, carry):
        cf, cr = carry
        rf = pl.multiple_of(i * 8, 8)
        rr = pl.multiple_of((nt - 1 - i) * 8, 8)
        af, bf = _tile_scan(a_s[0, pl.ds(rf, 8), :], b_s[0, pl.ds(rf, 8), :], False)
        ar, br = _tile_scan(a_s[1, pl.ds(rr, 8), :], b_s[1, pl.ds(rr, 8), :], True)
        hf = bf + af * cf
        hr = br + ar * cr
        hs_s[0, pl.ds(rf, 8), :] = hf
        hs_s[1, pl.ds(rr, 8), :] = hr
        return hf[7:8, :], hr[0:1, :]

    z = jnp.zeros((1, GROUP_W), F32)
    lax.fori_loop(0, nt, tile, (z, z), unroll=LRU_UNROLL if nt % LRU_UNROLL == 0 else 1)

    def epilogue(i, carry):
        rows = pl.ds(pl.multiple_of(i * tp, tp), tp)
        o_ref[0, rows, :] = ((hs_s[0, rows, :] + hs_s[1, rows, :]) * sg_s[rows, :]).astype(BF16)
        return carry

    lax.fori_loop(0, t_ // tp, epilogue, 0)


def _lru(h3, w, cw, cb, gw, gb, lam):
    b_, t_, _ = h3.shape
    return pl.pallas_call(
        functools.partial(_lru_kernel, t_=t_, tr=min(ROW_TILE, t_)),
        grid=(b_,),
        in_specs=[pl.BlockSpec((1, t_, D_MODEL), lambda b: (b, 0, 0)), _full(w), _full(cw), _full(cb),
                  _full(gw), _full(gb), _full(lam)],
        out_specs=pl.BlockSpec((1, t_, GROUP_W), lambda b: (b, 0, 0)),
        out_shape=jax.ShapeDtypeStruct((b_, t_, GROUP_W), BF16),
        scratch_shapes=[pltpu.VMEM((t_ + 2 * HALO, GROUP_W), F32),
                        pltpu.VMEM((2, t_, GROUP_W), F32), pltpu.VMEM((2, t_, GROUP_W), F32),
                        pltpu.VMEM((2, t_, GROUP_W), F32), pltpu.VMEM((t_, GROUP_W), F32)],
        compiler_params=_params(),
        name="lru",
    )(h3, w, cw, cb, gw, gb, lam)


def _rwkv_tables():
    bd = _head_blockdiag()
    t = np.arange(CHUNK)
    tri = np.stack([(t[:, None] >= t[None, :]), (t[:, None] <= t[None, :])]).astype(np.float32)
    strict = np.stack([(t[:, None] > t[None, :]), (t[:, None] < t[None, :])]).astype(np.float32)
    incl = tri.copy()
    eye = np.eye(CHUNK, dtype=np.float32)
    wide = lambda m: np.tile(m, (1,) * (m.ndim - 1) + (N_HEADS_G,))
    return dict(bd=bd, bdb=bd, tri2=np.concatenate([tri, tri], axis=-1), strict=wide(strict), incl=wide(incl),
                eye=wide(eye))


def _rwkv_kernel(h_ref, w_ref, sh_ref, w0_ref, wup_ref, a0_ref, aup_ref, kk_ref, ka_ref, rk_ref,
                 lng_ref, lnb_ref, bd_ref, bdb_ref, tri_ref, strict_ref, incl_ref, eye_ref, o_ref,
                 zs_s, r_s, v_s, sg_s, lg_s, pt_s, qd_s, kd_s, y_s, st_s, *, t_, tr):
    bd = bd_ref[...]
    bdb = bdb_ref[...]
    g = GROUP_W
    sw = RWKV_SHIFT_W
    cc = CHUNK
    bf = lambda x: x.astype(BF16)

    zhalo = jnp.zeros((HALO, sw), F32)
    zs_s[0:HALO, :] = zhalo
    zs_s[HALO + t_:, :] = zhalo

    tp = min(ROW_TILE, t_)

    def proj(i, carry):
        r0 = pl.multiple_of(i * tp, tp)
        z = _dot(h_ref[0, pl.ds(r0, tp), :], w_ref[...])
        zs_s[pl.ds(pl.multiple_of(HALO + r0, HALO), tp), :] = z[:, 0:sw]
        sg_s[pl.ds(r0, tp), :] = _silu(z[:, sw:])
        return carry

    lax.fori_loop(0, t_ // tp, proj, 0)

    ka1 = 1.0 - ka_ref[...]
    sh_self = 1.0 - sh_ref[0:1, :] - sh_ref[1:2, :]

    def prep(i, carry):
        r0 = pl.multiple_of(i * tr, tr)
        rows = pl.ds(r0, tr)

        def mixed(c0, c1):
            win = zs_s[pl.ds(r0, tr + 2 * HALO), c0:c1]
            nw = tr + 2 * HALO
            x = win[HALO:HALO + tr, :]
            prev = pltpu.roll(win, 1, axis=0)[HALO:HALO + tr, :]
            nxt = pltpu.roll(win, nw - 1, axis=0)[HALO:HALO + tr, :]
            return (sh_self[:, c0:c1] * x + sh_ref[0:1, c0:c1] * prev) + sh_ref[1:2, c0:c1] * nxt

        r_s[rows, :] = mixed(0, g)
        v_s[rows, :] = mixed(2 * g, 3 * g)
        k = mixed(g, 2 * g)
        lo = mixed(3 * g, sw)
        kk = k * kk_ref[...]
        ssq = _dot(bf(kk * kk), bdb)
        kk = kk * lax.rsqrt(jnp.maximum(ssq, 1e-24))
        dw = _dot(bf(jnp.tanh(lo)), wup_ref[...])
        da = _dot(bf(lo), aup_ref[...])
        for d in range(2):
            lw = -EXP_M05 * _sigmoid(w0_ref[d:d + 1, :] + dw[:, d * g:(d + 1) * g])
            a = _sigmoid(a0_ref[d:d + 1, :] + da[:, d * g:(d + 1) * g])
            kd_s[d, rows, :] = k * (ka1 + a * ka_ref[...])
            qd_s[d, rows, :] = kk * a
            lw_hi = bf(lw)
            lw_lo = bf(lw - lw_hi.astype(F32))
            lgs = [_dot(tri_ref[d], jnp.concatenate([lw_hi[c0:c0 + cc, :], lw_lo[c0:c0 + cc, :]], axis=0))
                   for c0 in range(0, tr, cc)]
            lg = jnp.concatenate(lgs, axis=0) if len(lgs) > 1 else lgs[0]
            lg_s[d, rows, :] = lg
            pt_s[d, rows, :] = kk * jnp.exp(lg - lw)
        return carry

    lax.fori_loop(0, t_ // tr, prep, 0)
    st_s[...] = jnp.zeros((2, g, g), F32)

    nc = t_ // cc
    ug = min(RWKV_GROUP, nc)
    rep = lambda x: jnp.concatenate([bf(x)] * N_HEADS_G, axis=0) * bdb
    stack = lambda a, b: bf(jnp.concatenate([a, b], axis=0))
    eye = eye_ref[...]

    def group(i, carry):
        ch = []
        for j in range(ug):
            for d in range(2):
                c = i * ug + j
                c = c if d == 0 else nc - 1 - c
                rows = pl.ds(pl.multiple_of(c * cc, cc), cc)
                lg = lg_s[d, rows, :]
                tot = lg[cc - 1:cc, :] if d == 0 else lg[0:1, :]
                g_inv = jnp.exp(-lg)
                g_end = jnp.exp(tot - lg)
                qdc = qd_s[d, rows, :]
                kdc = kd_s[d, rows, :]
                ch.append(dict(d=d, rows=rows, gtot=jnp.exp(tot), pt=pt_s[d, rows, :],
                               rt=r_s[rows, :] * jnp.exp(lg), qt=qdc * g_inv, kt=kdc * g_inv,
                               qe=qdc * g_end, ke=kdc * g_end, vc=v_s[rows, :]))
        for s in ch:
            pr = stack(s["pt"], s["rt"])
            a_q = _dot_nt(pr, rep(s["qt"]))
            a_k = _dot_nt(pr, rep(s["kt"]))
            strict = strict_ref[s["d"]]
            incl = incl_ref[s["d"]]
            s["a_rq"] = bf(a_q[cc:, :] * incl)
            s["a_k"] = stack(a_k[:cc, :] * strict, a_k[cc:, :] * incl)
            s["x"] = -(a_q[:cc, :] * strict)
            s["tm"] = eye + s["x"]
        for s in ch:
            s["x"] = _dot(bf(s["x"]), rep(s["x"]))
        for k in range(5):
            for s in ch:
                if k < 4:
                    both = _dot(stack(s["x"], s["tm"]), rep(s["x"]))
                    s["x"] = both[:cc, :]
                    s["tm"] = s["tm"] + both[cc:, :]
                else:
                    s["tm"] = s["tm"] + _dot(bf(s["tm"]), rep(s["x"]))
        for s in ch:
            s["tmb"] = bf(s["tm"])
            s["wm"] = _dot(s["tmb"], rep(s["pt"]))
            s["av"] = _dot(s["a_k"], rep(s["vc"]))
        for s in ch:
            s["u0"] = _dot(s["tmb"], rep(s["av"][:cc, :]))
            s["rm"] = bf(s["rt"] - _dot(s["a_rq"], rep(s["wm"])))
            s["mm"] = bf(_dot_tn(s["wm"], s["qe"]) * bd)
        for s in ch:
            s["y0"] = s["av"][cc:, :] - _dot(s["a_rq"], rep(s["u0"]))
            s["nn"] = _dot_tn(jnp.concatenate([s["vc"], -s["u0"]], axis=0),
                              jnp.concatenate([s["ke"], s["qe"]], axis=0)) * bd
        sts = [st_s[0], st_s[1]]
        for j in range(ug):
            for d in range(2):
                s = ch[2 * j + d]
                sb = bf(sts[d])
                upd = _dot(sb, s["mm"])
                y_s[d, s["rows"], :] = _dot_nt(s["rm"], sb) + s["y0"]
                sts[d] = sts[d] * s["gtot"] - upd + s["nn"]
        st_s[0] = sts[0]
        st_s[1] = sts[1]
        return carry

    lax.fori_loop(0, nc // ug, group, 0)

    def epilogue(i, carry):
        r0 = pl.multiple_of(i * tp, tp)
        rows = pl.ds(r0, tp)
        y = y_s[0, rows, :] + y_s[1, rows, :]
        mu = _dot(bf(y), bdb) * (1.0 / HEAD_DIM)
        yc = y - mu
        var = _dot(bf(yc * yc), bdb) * (1.0 / HEAD_DIM)
        ksum = kd_s[0, rows, :] + kd_s[1, rows, :]
        bonus = _dot(bf(r_s[rows, :] * ksum * rk_ref[...]), bdb) * v_s[rows, :]
        yn = yc * lax.rsqrt(var + GN_EPS) * lng_ref[...] + lnb_ref[...] + bonus
        o_ref[0, rows, :] = (yn * sg_s[rows, :]).astype(BF16)
        return carry

    lax.fori_loop(0, t_ // tp, epilogue, 0)


def _rwkv(h3, w, p, tabs):
    b_, t_, _ = h3.shape
    args = [h3, w, p["shift"], p["w0"], p["wup"], p["a0"], p["aup"], p["kk"], p["ka"], p["rk"], p["lng"],
            p["lnb"], tabs["bd"], tabs["bdb"], tabs["tri2"], tabs["strict"], tabs["incl"], tabs["eye"]]
    big = lambda: pltpu.VMEM((t_, GROUP_W), F32)
    big2 = lambda: pltpu.VMEM((2, t_, GROUP_W), F32)
    return pl.pallas_call(
        functools.partial(_rwkv_kernel, t_=t_, tr=min(128, t_)),
        grid=(b_,),
        in_specs=[pl.BlockSpec((1, t_, D_MODEL), lambda b: (b, 0, 0))] + [_full(a) for a in args[1:]],
        out_specs=pl.BlockSpec((1, t_, GROUP_W), lambda b: (b, 0, 0)),
        out_shape=jax.ShapeDtypeStruct((b_, t_, GROUP_W), BF16),
        scratch_shapes=[pltpu.VMEM((t_ + 2 * HALO, RWKV_SHIFT_W), F32),
                        big(), big(), big(), big2(), big2(), big2(), big2(), big2(),
                        pltpu.VMEM((2, GROUP_W, GROUP_W), F32)],
        compiler_params=_params(),
        name="rwkv",
    )(*args)


def _prep_layer(l, P):
    w_in = P["w_in"][l].astype(BF16)
    c0, c1, c2 = A_W, A_W + B_W, A_W + B_W + C_W
    pad = jnp.zeros((2, RWKV_DECAY_RANK, GROUP_W), F32)
    wup = jnp.concatenate([P["rwkv_w_up"][l], pad], axis=1)
    aup = jnp.concatenate([pad, P["rwkv_a_up"][l]], axis=1)
    cat2 = lambda m: jnp.concatenate([m[0], m[1]], axis=1).astype(BF16)
    gw = P["lru_gate_w"][l]
    blocks = []
    for d in range(2):
        for kk in range(2):
            blocks.append(jax.scipy.linalg.block_diag(*[gw[d, kk, n] for n in range(LRU_BLOCKS)]))
    row = lambda a: a.reshape(1, -1)
    return dict(
        wA=w_in[:, :c0], wB=w_in[:, c0:c1], wC=w_in[:, c1:c2], wD=w_in[:, c2:],
        w_out=P["w_out"][l].astype(BF16),
        rwkv=dict(shift=P["rwkv_shift"][l], w0=P["rwkv_w0"][l], wup=cat2(wup), a0=P["rwkv_a0"][l],
                  aup=cat2(aup), kk=row(P["rwkv_k_k"][l]), ka=row(P["rwkv_k_a"][l]),
                  rk=row(P["rwkv_r_k"][l]), lng=row(P["rwkv_ln_g"][l]), lnb=row(P["rwkv_ln_b"][l])),
        qw=jnp.tile(P["attn_q_norm"][l], 2).reshape(1, -1), kw=jnp.tile(P["attn_k_norm"][l], 2).reshape(1, -1),
        cw=P["lru_conv_w"][l], cb=row(P["lru_conv_b"][l]),
        gw=jnp.concatenate(blocks, axis=1).astype(BF16), gb=P["lru_gate_b"][l].reshape(1, -1),
        lam=P["lru_lambda"][l], sink=P["swa_sink"][l],
    )


def _tables(t_):
    cos, sin = _rope_tables(t_)
    qi = np.arange(Q_BLOCK)[:, None]
    ki = np.arange(3 * Q_BLOCK)[None, :]
    dist = np.abs(ki - WINDOW - qi).astype(np.float32)
    slopes = np.exp2(-8.0 * np.arange(1, N_HEADS_G + 1, dtype=np.float32) / N_HEADS_G).astype(np.float32)
    alibi = np.where(dist[None] <= WINDOW, -slopes[:, None, None] * dist[None], NEG).astype(np.float32)
    first = np.where(ki[None] < Q_BLOCK, NEG, alibi).astype(np.float32)
    last = np.where(ki[None] >= 2 * Q_BLOCK, NEG, alibi).astype(np.float32)
    wbias = np.ascontiguousarray(np.stack([alibi, first, last]).transpose(0, 1, 3, 2))
    tabs = dict(cos=cos, sin=sin, seg128=_head_blockdiag()[:128, :128], wbias=wbias)
    tabs.update(_rwkv_tables())
    out = {k: jnp.asarray(v) for k, v in tabs.items()}
    for k in ("bdb", "tri2", "seg128"):
        out[k] = out[k].astype(BF16)
    return out


def _trunk(x, P, layers):
    b_, t_, _ = x.shape
    tabs = _tables(t_)
    x2 = x.reshape(b_ * t_, D_MODEL)
    h2 = _norm(x2, P["norm_g"][0].reshape(1, -1))
    for l in range(DEPTH):
        lp = layers[l]
        h3 = h2.reshape(b_, t_, D_MODEL)
        oa = _rwkv(h3, lp["wA"], lp["rwkv"], tabs)
        ob = _gattn(h3, lp["wB"], lp["qw"], lp["kw"], tabs)
        oc = _lru(h3, lp["wC"], lp["cw"], lp["cb"], lp["gw"], lp["gb"], lp["lam"])
        od = _wattn(h3, lp["wD"], lp["sink"], tabs)
        flat = lambda o: o.reshape(b_ * t_, GROUP_W)
        final = l == DEPTH - 1
        g_next = (P["final_g"] if final else P["norm_g"][l + 1]).reshape(1, -1)
        outs = _out_proj(flat(oa), flat(ob), flat(oc), flat(od), lp["w_out"], x2, g_next, final)
        if final:
            x2 = outs[0]
        else:
            x2, h2 = outs
    return x2.reshape(b_, t_, D_MODEL)


def kernel(x_prompt, x_sample, norm_g, w_in, w_out, rwkv_shift, rwkv_w0, rwkv_w_up, rwkv_a0, rwkv_a_up,
           rwkv_k_k, rwkv_k_a, rwkv_r_k, rwkv_ln_g, rwkv_ln_b, attn_q_norm, attn_k_norm, lru_conv_w,
           lru_conv_b, lru_gate_w, lru_gate_b, lru_lambda, swa_sink, final_g):
    P = dict(norm_g=norm_g, w_in=w_in, w_out=w_out, rwkv_shift=rwkv_shift, rwkv_w0=rwkv_w0,
             rwkv_w_up=rwkv_w_up, rwkv_a0=rwkv_a0, rwkv_a_up=rwkv_a_up, rwkv_k_k=rwkv_k_k,
             rwkv_k_a=rwkv_k_a, rwkv_r_k=rwkv_r_k, rwkv_ln_g=rwkv_ln_g, rwkv_ln_b=rwkv_ln_b,
             attn_q_norm=attn_q_norm, attn_k_norm=attn_k_norm, lru_conv_w=lru_conv_w,
             lru_conv_b=lru_conv_b, lru_gate_w=lru_gate_w, lru_gate_b=lru_gate_b,
             lru_lambda=lru_lambda, swa_sink=swa_sink, final_g=final_g)
    layers = [_prep_layer(l, P) for l in range(DEPTH)]
    return _trunk(x_prompt, P, layers), _trunk(x_sample, P, layers)
```
